```python
import numpy as np
import jax
import jax.numpy as jnp
from jax import lax

D_MODEL = 1024
BATCH = 4
SEQ = 4096
DEPTH = 4
DEC_BATCH = 16
DEC_SEQ = 4096
PAST_LEN = 128

N_MIXERS = 4
BRANCH = 1024
PLE_DIM = 256
GRID_W = 64
EPS = 1e-6

FN_GROUPS = 8
FN_GDIM = BRANCH // FN_GROUPS

NA_HEADS = 32
NA_HDIM = BRANCH // NA_HEADS
NA_KH_MAX = 8
NA_KW = 16
NA_QCB = 16
NA_KCB = 32

MLA_HEADS = 8
MLA_NOPE = 128
MLA_ROPE = 64
MLA_V = BRANCH // MLA_HEADS
MLA_Q_LORA = 384
MLA_KV_LORA = 256
MLA_QBLOCK = 128
ROPE_THETA = 10000.0

HG_EXPAND = 128
HG_HEADS = BRANCH // HG_EXPAND
HG_DV = BRANCH // HG_HEADS
HG_CHUNK = 64

kernel_name = "hybrid_bidir_fnet_natten_mla_hgrn2"


def rmsnorm(x, g):
    xf = x.astype(jnp.float32)
    y = xf * lax.rsqrt(jnp.mean(xf * xf, axis=-1, keepdims=True) + EPS)
    return (y * g.astype(jnp.float32)).astype(x.dtype)


def fourier_mixer(h, w_in, w_mix, w_out):
    b, s, _ = h.shape
    u, z = jnp.split(h @ w_in, 2, axis=-1)
    u = u.reshape(b, s, FN_GROUPS, FN_GDIM).astype(jnp.float32)
    f = jnp.fft.fft2(u, axes=(1, 3), norm="ortho").real.astype(h.dtype)
    y = jnp.einsum('bsgc,gcd->bsgd', f, w_mix).reshape(b, s, BRANCH)
    return (y * jax.nn.silu(z)) @ w_out


def na_tables(rows):
    kh = min(NA_KH_MAX, rows)
    r = np.arange(rows)
    row_start = np.clip(r - kh // 2, 0, rows - kh)
    row_off = row_start[:, None] + np.arange(kh)[None, :] - r[:, None] + (NA_KH_MAX - 1)
    ncb = GRID_W // NA_QCB
    j = np.arange(ncb)
    kc_start = np.clip(j * NA_QCB - NA_KW // 2, 0, GRID_W - NA_KCB)
    qcol = j[:, None] * NA_QCB + np.arange(NA_QCB)[None, :]
    kcol = kc_start[:, None] + np.arange(NA_KCB)[None, :]
    win = np.clip(qcol - NA_KW // 2, 0, GRID_W - NA_KW)
    col_mask = (kcol[:, None, :] >= win[:, :, None]) & (kcol[:, None, :] < win[:, :, None] + NA_KW)
    col_off = np.clip(kcol[:, None, :] - qcol[:, :, None] + (NA_KW - 1), 0, 2 * NA_KW - 2)
    return kh, row_start, row_off, kcol, col_mask, col_off


def na_mixer(h, w_in, rpb, w_out):
    b, s, _ = h.shape
    rows = s // GRID_W
    kh, row_start, row_off, kcol, col_mask, col_off = na_tables(rows)
    ncb = GRID_W // NA_QCB
    q, k, v, z = jnp.split(h @ w_in, 4, axis=-1)
    grid = lambda t: t.reshape(b, rows, GRID_W, NA_HEADS, NA_HDIM)
    q = grid(q) * (NA_HDIM ** -0.5)
    k, v = grid(k), grid(v)
    qb = jnp.moveaxis(q.reshape(b, rows, ncb, NA_QCB, NA_HEADS, NA_HDIM), 1, 0)
    mask = jnp.asarray(col_mask)[:, :, None, :]
    col_off_j = jnp.asarray(col_off)

    def one_row(args):
        q_r, rs, ro = args
        k_rows = lax.dynamic_slice_in_dim(k, rs, kh, axis=1)
        v_rows = lax.dynamic_slice_in_dim(v, rs, kh, axis=1)
        k_blk = k_rows[:, :, kcol]
        v_blk = v_rows[:, :, kcol]
        sc = jnp.einsum('bjqhd,brjkhd->bhjqrk', q_r, k_blk).astype(jnp.float32)
        bias = rpb[:, ro, :][:, :, col_off_j]
        sc = sc + jnp.transpose(bias, (0, 2, 3, 1, 4)).astype(jnp.float32)
        sc = jnp.where(mask, sc, -jnp.inf)
        shp = sc.shape
        p = jax.nn.softmax(sc.reshape(shp[:-2] + (kh * NA_KCB,)), axis=-1).reshape(shp)
        return jnp.einsum('bhjqrk,brjkhd->bjqhd', p.astype(v.dtype), v_blk)

    o = lax.map(one_row, (qb, jnp.asarray(row_start, jnp.int32), jnp.asarray(row_off, jnp.int32)))
    o = jnp.moveaxis(o, 0, 1).reshape(b, s, BRANCH)
    return (o * jax.nn.silu(z)) @ w_out


def apply_rope(x, s):
    half = MLA_ROPE // 2
    inv = ROPE_THETA ** (-jnp.arange(half, dtype=jnp.float32) / half)
    ang = jnp.arange(s, dtype=jnp.float32)[:, None] * inv[None, :]
    ang = ang.reshape((1, s) + (1,) * (x.ndim - 3) + (half,))
    cos, sin = jnp.cos(ang), jnp.sin(ang)
    xf = x.astype(jnp.float32)
    x1, x2 = xf[..., :half], xf[..., half:]
    return jnp.concatenate([x1 * cos - x2 * sin, x1 * sin + x2 * cos], axis=-1).astype(x.dtype)


def mla_mixer(h, w_in, g_q, w_uq, g_kv, w_ukv, w_out):
    b, s, _ = h.shape
    c_q, c_kv, k_pe, z = jnp.split(h @ w_in, [MLA_Q_LORA, MLA_Q_LORA + MLA_KV_LORA,
                                              MLA_Q_LORA + MLA_KV_LORA + MLA_ROPE], axis=-1)
    scale = (MLA_NOPE + MLA_ROPE) ** -0.5
    q = (rmsnorm(c_q, g_q) @ w_uq).reshape(b, s, MLA_HEADS, MLA_NOPE + MLA_ROPE) * scale
    q_nope, q_pe = q[..., :MLA_NOPE], apply_rope(q[..., MLA_NOPE:], s)
    k_pe = apply_rope(k_pe, s)
    kv = (rmsnorm(c_kv, g_kv) @ w_ukv).reshape(b, s, MLA_HEADS, MLA_NOPE + MLA_V)
    k_nope, v = kv[..., :MLA_NOPE], kv[..., MLA_NOPE:]
    nblk = s // MLA_QBLOCK
    qn = jnp.moveaxis(q_nope.reshape(b, nblk, MLA_QBLOCK, MLA_HEADS, MLA_NOPE), 1, 0)
    qp = jnp.moveaxis(q_pe.reshape(b, nblk, MLA_QBLOCK, MLA_HEADS, MLA_ROPE), 1, 0)

    def block(args):
        qn_i, qp_i = args
        sc = (jnp.einsum('bqhd,bkhd->bhqk', qn_i, k_nope).astype(jnp.float32)
              + jnp.einsum('bqhr,bkr->bhqk', qp_i, k_pe).astype(jnp.float32))
        p = jax.nn.softmax(sc, axis=-1).astype(v.dtype)
        return jnp.einsum('bhqk,bkhd->bqhd', p, v)

    o = lax.map(block, (qn, qp))
    o = jnp.moveaxis(o, 0, 1).reshape(b, s, BRANCH)
    return (o * jax.nn.silu(z)) @ w_out


def gla_chunk_scan(q, k, v, logf):
    b, s, nh, dk = q.shape
    dv = v.shape[-1]
    n = s // HG_CHUNK
    to_chunks = lambda t: jnp.moveaxis(t.reshape(b, n, HG_CHUNK, nh, t.shape[-1]), 1, 0)
    causal = jnp.tril(jnp.ones((HG_CHUNK, HG_CHUNK), bool))[None, :, :, None, None]

    def step(S, inp):
        qc, kc, vc, gc = inp
        B = jnp.cumsum(gc, axis=1)
        decay = jnp.exp(jnp.where(causal, B[:, :, None] - B[:, None, :], -jnp.inf))
        A = jnp.einsum('bthc,btshc,bshc->bhts', qc, decay, kc)
        o = jnp.einsum('bhts,bshv->bthv', A, vc) + jnp.einsum('bthc,bhcv->bthv', qc * jnp.exp(B), S)
        BL = B[:, -1]
        S_new = jnp.exp(BL)[..., None] * S + jnp.einsum('bshc,bshv->bhcv', kc * jnp.exp(BL[:, None] - B), vc)
        return S_new, o

    S0 = jnp.zeros((b, nh, dk, dv), jnp.float32)
    _, o = lax.scan(step, S0, (to_chunks(q), to_chunks(k), to_chunks(v), to_chunks(logf)))
    return jnp.moveaxis(o, 0, 1).reshape(b, s, nh, dv)


def hgrn2_mixer(h, w_in, lower, g_out, w_out):
    b, s, _ = h.shape
    q, f_fw, f_bw, i, z = jnp.split(h @ w_in, 5, axis=-1)
    heads = lambda t: t.astype(jnp.float32).reshape(b, s, HG_HEADS, -1)
    q = jax.nn.silu(heads(q))
    i = heads(i)

    def direction(f_raw, lb):
        f = heads(lb + (1.0 - lb) * jax.nn.sigmoid(f_raw.astype(jnp.float32)))
        return jnp.log(f), 1.0 - f

    lf_f, k_f = direction(f_fw, lower[0])
    lf_b, k_b = direction(f_bw, lower[1])
    o_f = gla_chunk_scan(q, k_f, i, lf_f)
    flip = lambda t: jnp.flip(t, axis=1)
    o_b = flip(gla_chunk_scan(flip(q), flip(k_b), flip(i), flip(lf_b)))
    o = o_f + o_b
    o = o * lax.rsqrt(jnp.mean(o * o, axis=-1, keepdims=True) + EPS)
    o = (o.reshape(b, s, BRANCH) * g_out.astype(jnp.float32)).astype(h.dtype)
    return (o * jax.nn.silu(z)) @ w_out


def trunk(x, p, norm_g, fn_w_in, fn_w_mix, fn_w_out, na_w_in, na_rpb, na_w_out,
          mla_w_in, mla_g_q, mla_w_uq, mla_g_kv, mla_w_ukv, mla_w_out,
          hg_w_in, hg_lb_raw, hg_g_out, hg_w_out, ple_w, ple_gate_w, final_g):
    sm = jax.nn.softmax(hg_lb_raw.astype(jnp.float32), axis=0)
    lower = jnp.cumsum(sm, axis=0) - sm[0]
    for li in range(DEPTH):
        m, j = li % N_MIXERS, li // N_MIXERS
        h = rmsnorm(x, norm_g[li])
        if m == 0:
            y = fourier_mixer(h, fn_w_in[j], fn_w_mix[j], fn_w_out[j])
        elif m == 1:
            y = na_mixer(h, na_w_in[j], na_rpb[j], na_w_out[j])
        elif m == 2:
            y = mla_mixer(h, mla_w_in[j], mla_g_q[j], mla_w_uq[j], mla_g_kv[j], mla_w_ukv[j], mla_w_out[j])
        else:
            y = hgrn2_mixer(h, hg_w_in[j], lower[li], hg_g_out[j], hg_w_out[j])
        x = x + y
        x = x + jax.nn.sigmoid(x @ ple_gate_w[li]) * (p[li] @ ple_w[li])
    return rmsnorm(x, final_g)


def setup_inputs(seed: int = 0) -> dict:
    key = jax.random.key(seed)
    ks = iter(jax.random.split(key, 32))
    nrm = lambda shape, scale: jax.random.normal(next(ks), shape, jnp.float32) * scale
    gain = lambda shape: 1.0 + nrm(shape, 0.02)
    nA, nB, nC, nD = (len(range(m, DEPTH, N_MIXERS)) for m in range(N_MIXERS))
    D = D_MODEL
    return {
        "x_prompt": nrm((BATCH, SEQ, D), 1.0),
        "x_sample": nrm((DEC_BATCH, DEC_SEQ, D), 1.0),
        "p_prompt": nrm((DEPTH, BATCH, SEQ, PLE_DIM), 1.0),
        "p_sample": nrm((DEPTH, DEC_BATCH, DEC_SEQ, PLE_DIM), 1.0),
        "norm_g": gain((DEPTH, D)),
        "fn_w_in": nrm((nA, D, 2 * BRANCH), D ** -0.5),
        "fn_w_mix": nrm((nA, FN_GROUPS, FN_GDIM, FN_GDIM), FN_GDIM ** -0.5),
        "fn_w_out": nrm((nA, BRANCH, D), BRANCH ** -0.5),
        "na_w_in": nrm((nB, D, 4 * BRANCH), D ** -0.5),
        "na_rpb": nrm((nB, NA_HEADS, 2 * NA_KH_MAX - 1, 2 * NA_KW - 1), 0.1),
        "na_w_out": nrm((nB, BRANCH, D), BRANCH ** -0.5),
        "mla_w_in": nrm((nC, D, MLA_Q_LORA + MLA_KV_LORA + MLA_ROPE + BRANCH), D ** -0.5),
        "mla_g_q": gain((nC, MLA_Q_LORA)),
        "mla_w_uq": nrm((nC, MLA_Q_LORA, MLA_HEADS * (MLA_NOPE + MLA_ROPE)), MLA_Q_LORA ** -0.5),
        "mla_g_kv": gain((nC, MLA_KV_LORA)),
        "mla_w_ukv": nrm((nC, MLA_KV_LORA, MLA_HEADS * (MLA_NOPE + MLA_V)), MLA_KV_LORA ** -0.5),
        "mla_w_out": nrm((nC, BRANCH, D), BRANCH ** -0.5),
        "hg_w_in": nrm((nD, D, 5 * BRANCH), D ** -0.5),
        "hg_lb_raw": nrm((DEPTH, 2, BRANCH), 0.1),
        "hg_g_out": gain((nD, BRANCH)),
        "hg_w_out": nrm((nD, BRANCH, D), BRANCH ** -0.5),
        "ple_w": nrm((DEPTH, PLE_DIM, D), PLE_DIM ** -0.5),
        "ple_gate_w": nrm((DEPTH, D, D), D ** -0.5),
        "final_g": gain((D,)),
    }


def reference(x_prompt, x_sample, p_prompt, p_sample, norm_g, fn_w_in, fn_w_mix, fn_w_out,
              na_w_in, na_rpb, na_w_out, mla_w_in, mla_g_q, mla_w_uq, mla_g_kv, mla_w_ukv, mla_w_out,
              hg_w_in, hg_lb_raw, hg_g_out, hg_w_out, ple_w, ple_gate_w, final_g):
    y_prompt = trunk(x_prompt, p_prompt, norm_g, fn_w_in, fn_w_mix, fn_w_out, na_w_in, na_rpb, na_w_out,
                     mla_w_in, mla_g_q, mla_w_uq, mla_g_kv, mla_w_ukv, mla_w_out,
                     hg_w_in, hg_lb_raw, hg_g_out, hg_w_out, ple_w, ple_gate_w, final_g)
    y_sample = trunk(x_sample, p_sample, norm_g, fn_w_in, fn_w_mix, fn_w_out, na_w_in, na_rpb, na_w_out,
                     mla_w_in, mla_g_q, mla_w_uq, mla_g_kv, mla_w_ukv, mla_w_out,
                     hg_w_in, hg_lb_raw, hg_g_out, hg_w_out, ple_w, ple_gate_w, final_g)
    return (y_prompt, y_sample)
```

```python
import functools
import math

import numpy as np
import jax
import jax.numpy as jnp
from jax import lax
from jax.experimental import pallas as pl
from jax.experimental.pallas import tpu as pltpu

F32 = jnp.float32
BF16 = jnp.bfloat16

D_MODEL = 1024
DEPTH = 4
BRANCH = 1024
PLE_DIM = 256
GRID_W = 64
EPS = 1e-6
LANE = 128
VMEM_LIMIT = 56 * 1024 * 1024

FN_GROUPS = 8
FN_GDIM = BRANCH // FN_GROUPS

NA_HEADS = 32
NA_HDIM = BRANCH // NA_HEADS
NA_KH = 8
NA_KW = 16
NA_HPG = LANE // NA_HDIM
NA_GROUPS = NA_HEADS // NA_HPG
NA_NEG = -1e30

MLA_HEADS = 8
MLA_NOPE = 128
MLA_ROPE = 64
MLA_V = BRANCH // MLA_HEADS
MLA_Q_LORA = 384
MLA_KV_LORA = 256
MLA_QK = MLA_NOPE + 2 * MLA_ROPE
ROPE_THETA = 10000.0

HG_HEADS = 8
HG_DK = BRANCH // HG_HEADS
HG_CHUNK = 64
HG_LEVELS = (64, 32, 16, 8, 4, 2)


def _cparams(sem):
    return pltpu.CompilerParams(dimension_semantics=sem, vmem_limit_bytes=VMEM_LIMIT)


def _rmsnorm(x, g):
    ms = jnp.mean(x * x, axis=-1, keepdims=True)
    return x * lax.rsqrt(ms + EPS) * g


def _sigmoid(x):
    return 1.0 / (1.0 + jnp.exp(-x))


def _silu(x):
    return x * _sigmoid(x)


def _dot(a, b):
    return jnp.dot(a, b, preferred_element_type=F32)


def _dot_nt(a, b):
    return lax.dot_general(a, b, (((1,), (1,)), ((), ())), preferred_element_type=F32)


def _dot_tn(a, b):
    return lax.dot_general(a, b, (((0,), (0,)), ((), ())), preferred_element_type=F32)


def _tok_spec(tm, width):
    return pl.BlockSpec((None, tm, width), lambda b, i: (b, i, 0))


def _full_spec(shape):
    nd = len(shape)
    return pl.BlockSpec(shape, lambda *_: (0,) * nd)


def _fn_in_kernel(x_ref, g_ref, w_ref, u_ref, z_ref):
    h = _rmsnorm(x_ref[...], g_ref[...]).astype(BF16)
    u_ref[...] = _dot(h, w_ref[:, :BRANCH]).astype(BF16)
    z_ref[...] = _dot(h, w_ref[:, BRANCH:]).astype(BF16)


def _fn_in(x, g, w, tm):
    b, s, d = x.shape
    return pl.pallas_call(
        _fn_in_kernel,
        grid=(b, s // tm),
        in_specs=[_tok_spec(tm, d), _full_spec((1, d)), _full_spec((d, 2 * BRANCH))],
        out_specs=[_tok_spec(tm, BRANCH)] * 2,
        out_shape=[jax.ShapeDtypeStruct((b, s, BRANCH), BF16)] * 2,
        compiler_params=_cparams(("parallel", "parallel")),
        name="fn_in",
    )(x, g, w)


def _fn_fold_kernel(c_ref, s_ref, w_ref, o_ref, *, scale):
    w = w_ref[...]
    hi = lax.Precision.HIGHEST
    wc = jnp.dot(c_ref[...], w, precision=hi, preferred_element_type=F32)
    ws = jnp.dot(s_ref[...], w, precision=hi, preferred_element_type=F32)
    o_ref[:FN_GDIM, :] = (scale * wc).astype(BF16)
    o_ref[FN_GDIM:, :] = (-scale * ws).astype(BF16)


def _fn_fold(w_mix, s):
    c = jnp.arange(FN_GDIM, dtype=jnp.int32)
    ang = ((c[:, None] * c[None, :]) % FN_GDIM).astype(F32) * (2.0 * math.pi / FN_GDIM)
    scale = 1.0 / math.sqrt(s * FN_GDIM)
    gspec = pl.BlockSpec((None, FN_GDIM, FN_GDIM), lambda g: (g, 0, 0))
    return pl.pallas_call(
        functools.partial(_fn_fold_kernel, scale=scale),
        grid=(FN_GROUPS,),
        in_specs=[_full_spec((FN_GDIM, FN_GDIM))] * 2 + [gspec],
        out_specs=pl.BlockSpec((None, 2 * FN_GDIM, FN_GDIM), lambda g: (g, 0, 0)),
        out_shape=jax.ShapeDtypeStruct((FN_GROUPS, 2 * FN_GDIM, FN_GDIM), BF16),
        name="fn_fold",
    )(jnp.cos(ang), jnp.sin(ang), w_mix)


def _dft_mats(s, tk):
    k = jnp.arange(s, dtype=jnp.int32)
    ang = ((k[:, None] * k[None, :]) % s).astype(F32) * (2.0 * math.pi / s)
    c = jnp.cos(ang).astype(BF16).reshape(s // tk, tk, s)
    sn = jnp.sin(ang).astype(BF16).reshape(s // tk, tk, s)
    return jnp.concatenate([c, sn], axis=1)


def _fn_dft_kernel(m_ref, u_ref, z_ref, w2_ref, o_ref, *, tk):
    r = _dot(m_ref[...], u_ref[...])
    a = r[:tk].astype(BF16)
    bm = r[tk:].astype(BF16)
    for g in range(FN_GROUPS):
        sl = slice(g * FN_GDIM, (g + 1) * FN_GDIM)
        ab = jnp.concatenate([a[:, sl], bm[:, sl]], axis=1)
        y = _dot(ab, w2_ref[g])
        o_ref[:, sl] = (y * _silu(z_ref[:, sl].astype(F32))).astype(BF16)


def _fn_dft(mats, u, z, w2, tk):
    b, s, _ = u.shape
    return pl.pallas_call(
        functools.partial(_fn_dft_kernel, tk=tk),
        grid=(b, s // tk),
        in_specs=[pl.BlockSpec((None, 2 * tk, s), lambda bb, j: (j, 0, 0)),
                  pl.BlockSpec((None, s, BRANCH), lambda bb, j: (bb, 0, 0)),
                  _tok_spec(tk, BRANCH),
                  _full_spec((FN_GROUPS, 2 * FN_GDIM, FN_GDIM))],
        out_specs=_tok_spec(tk, BRANCH),
        out_shape=jax.ShapeDtypeStruct((b, s, BRANCH), BF16),
        compiler_params=_cparams(("parallel", "arbitrary")),
        name="fn_dft",
    )(mats, u, z, w2)


def _out_kernel(x_ref, o_ref, p_ref, wo_ref, gw_ref, pw_ref, *rest, final):
    y_ref = rest[-1]
    x1 = x_ref[...] + _dot(o_ref[...], wo_ref[...])
    gate = _sigmoid(_dot(x1.astype(BF16), gw_ref[...]))
    x2 = x1 + gate * _dot(p_ref[...].astype(BF16), pw_ref[...])
    if final:
        x2 = _rmsnorm(x2, rest[0][...])
    y_ref[...] = x2


def _out_proj(x, o, p, li, wo, gw, pw, final_g, tm):
    b, s, d = x.shape
    final = final_g is not None
    in_specs = [_tok_spec(tm, d), _tok_spec(tm, BRANCH),
                pl.BlockSpec((None, None, tm, PLE_DIM), lambda bb, i: (li, bb, i, 0)),
                _full_spec((BRANCH, d)), _full_spec((d, d)), _full_spec((PLE_DIM, d))]
    args = [x, o, p, wo, gw, pw]
    if final:
        in_specs.append(_full_spec((1, d)))
        args.append(final_g)
    return pl.pallas_call(
        functools.partial(_out_kernel, final=final),
        grid=(b, s // tm),
        in_specs=in_specs,
        out_specs=_tok_spec(tm, d),
        out_shape=jax.ShapeDtypeStruct((b, s, d), F32),
        compiler_params=_cparams(("parallel", "parallel")),
        name="out_proj",
    )(*args)


def _na_in_kernel(x_ref, g_ref, w_ref, wkt_ref, q_ref, kt_ref, v_ref, z_ref):
    h = _rmsnorm(x_ref[...], g_ref[...]).astype(BF16)
    q_ref[...] = (_dot(h, w_ref[:, :BRANCH]) * (NA_HDIM ** -0.5)).astype(BF16)
    kt_ref[...] = _dot_nt(wkt_ref[...], h).astype(BF16)
    v_ref[...] = _dot(h, w_ref[:, 2 * BRANCH:3 * BRANCH]).astype(BF16)
    z_ref[...] = _dot(h, w_ref[:, 3 * BRANCH:]).astype(BF16)


def _na_in(x, g, w, wkt, tm):
    b, s, d = x.shape
    tok = jax.ShapeDtypeStruct((b, s, BRANCH), BF16)
    return pl.pallas_call(
        _na_in_kernel,
        grid=(b, s // tm),
        in_specs=[_tok_spec(tm, d), _full_spec((1, d)), _full_spec((d, 4 * BRANCH)),
                  _full_spec((BRANCH, d))],
        out_specs=[_tok_spec(tm, BRANCH),
                   pl.BlockSpec((None, BRANCH, tm), lambda bb, i: (bb, 0, i)),
                   _tok_spec(tm, BRANCH), _tok_spec(tm, BRANCH)],
        out_shape=[tok, jax.ShapeDtypeStruct((b, BRANCH, s), BF16), tok, tok],
        compiler_params=_cparams(("parallel", "parallel")),
        name="na_in",
    )(x, g, w, wkt)


def _na_bias_table(rpb):
    d = np.arange(NA_KH)
    i = np.arange(NA_KH)
    ro = i[None, :] - d[:, None] + (NA_KH - 1)
    c = np.arange(GRID_W)
    kc = np.arange(GRID_W)
    win = np.clip(c - NA_KW // 2, 0, GRID_W - NA_KW)
    valid = (kc[None, :] >= win[:, None]) & (kc[None, :] < win[:, None] + NA_KW)
    co = np.clip(kc[None, :] - c[:, None] + (NA_KW - 1), 0, 2 * NA_KW - 2)
    bias = rpb[:, ro[:, :, None, None], co[None, None, :, :]]
    bias = jnp.where(jnp.asarray(valid)[None, None, None], bias, NA_NEG)
    bias = jnp.transpose(bias, (0, 1, 3, 2, 4))
    bias = bias.reshape(NA_GROUPS, NA_HPG, NA_KH, GRID_W, NA_KH * GRID_W)
    bias = jnp.transpose(bias, (0, 2, 1, 3, 4))
    return bias.reshape(NA_GROUPS, NA_KH, NA_HPG * GRID_W, NA_KH * GRID_W).astype(F32)


def _na_attn_kernel(q_ref, kt_ref, v_ref, z_ref, t_ref, o_ref, *, rows):
    lane = lax.broadcasted_iota(jnp.int32, (GRID_W, LANE), 1)
    head_masks = [(lane >= hh * NA_HDIM) & (lane < (hh + 1) * NA_HDIM) for hh in range(NA_HPG)]
    nkeys = NA_KH * GRID_W

    def body(r, carry):
        rs = jnp.clip(r - NA_KH // 2, 0, rows - NA_KH)
        d = r - rs
        par = rs & 1
        q0 = pl.multiple_of(r * GRID_W, GRID_W)
        q_r = q_ref[pl.ds(q0, GRID_W), :]
        zero = jnp.zeros_like(q_r)
        qs = jnp.concatenate([jnp.where(m, q_r, zero) for m in head_masks], axis=0)
        k0 = pl.multiple_of((rs - par) * GRID_W, LANE)
        kwin = kt_ref[par, :, pl.ds(k0, nkeys)]
        sc = _dot(qs, kwin) + t_ref[d]
        m = jnp.max(sc, axis=-1, keepdims=True)
        p = jnp.exp(sc - m)
        l = jnp.sum(p, axis=-1, keepdims=True)
        v0 = pl.multiple_of(rs * GRID_W, GRID_W)
        vwin = v_ref[pl.ds(v0, nkeys), :]
        o = _dot(p.astype(BF16), vwin) * (1.0 / l)
        acc = jnp.zeros((GRID_W, LANE), F32)
        for hh in range(NA_HPG):
            acc = jnp.where(head_masks[hh], o[hh * GRID_W:(hh + 1) * GRID_W], acc)
        zz = z_ref[pl.ds(q0, GRID_W), :].astype(F32)
        o_ref[pl.ds(q0, GRID_W), :] = (acc * _silu(zz)).astype(BF16)
        return carry

    lax.fori_loop(0, rows, body, 0)


def _na_attn(q, kt2, v, z, table):
    b, s, _ = q.shape
    rows = s // GRID_W
    assert rows >= NA_KH
    grp = pl.BlockSpec((None, s, LANE), lambda g, bb: (bb, 0, g))
    return pl.pallas_call(
        functools.partial(_na_attn_kernel, rows=rows),
        grid=(NA_GROUPS, b),
        in_specs=[grp,
                  pl.BlockSpec((None, 2, LANE, s), lambda g, bb: (bb, 0, g, 0)),
                  grp, grp,
                  pl.BlockSpec((None, NA_KH, NA_HPG * GRID_W, NA_KH * GRID_W),
                               lambda g, bb: (g, 0, 0, 0))],
        out_specs=grp,
        out_shape=jax.ShapeDtypeStruct((b, s, BRANCH), BF16),
        compiler_params=_cparams(("parallel", "parallel")),
        name="na_attn",
    )(q, kt2, v, z, table)


def _mla_in_kernel(x_ref, g_ref, w1_ref, gq_ref, wuq_ref, gkv_ref, wuk_ref, wuv_ref,
                   rq_ref, rk_ref, q_ref, k_ref, v_ref, z_ref):
    h = _rmsnorm(x_ref[...], g_ref[...]).astype(BF16)
    c0, c1, c2, c3 = MLA_Q_LORA, MLA_Q_LORA + MLA_KV_LORA, MLA_Q_LORA + MLA_KV_LORA + LANE, \
        MLA_Q_LORA + MLA_KV_LORA + 2 * LANE
    cq = _dot(h, w1_ref[:, :c0])
    ckv = _dot(h, w1_ref[:, c0:c1])
    kp = _dot(h, w1_ref[:, c1:c2])
    kr = _dot(h, w1_ref[:, c2:c3])
    z_ref[...] = _dot(h, w1_ref[:, c3:]).astype(BF16)
    cqn = _rmsnorm(cq, gq_ref[...]).astype(BF16)
    ckvn = _rmsnorm(ckv, gkv_ref[...]).astype(BF16)
    rq = rq_ref[...]
    for hh in range(MLA_HEADS):
        sl = slice(hh * MLA_QK, (hh + 1) * MLA_QK)
        q_ref[:, sl] = (_dot(cqn, wuq_ref[:, sl]) * rq).astype(BF16)
    krr = (kp * rk_ref[:, :LANE] + kr * rk_ref[:, LANE:]).astype(BF16)
    kn = _dot(ckvn, wuk_ref[...]).astype(BF16)
    for hh in range(MLA_HEADS):
        k_ref[:, hh * MLA_QK:hh * MLA_QK + MLA_NOPE] = kn[:, hh * MLA_NOPE:(hh + 1) * MLA_NOPE]
        k_ref[:, hh * MLA_QK + MLA_NOPE:(hh + 1) * MLA_QK] = krr
    v_ref[...] = _dot(ckvn, wuv_ref[...]).astype(BF16)


def _mla_in(x, g, w1, gq, wuq, gkv, wuk, wuv, rq, rk, tm):
    b, s, d = x.shape
    n1 = w1.shape[1]
    qk = MLA_HEADS * MLA_QK
    rope_spec = pl.BlockSpec((tm, 2 * LANE), lambda bb, i: (i, 0))
    return pl.pallas_call(
        _mla_in_kernel,
        grid=(b, s // tm),
        in_specs=[_tok_spec(tm, d), _full_spec((1, d)), _full_spec((d, n1)),
                  _full_spec((1, MLA_Q_LORA)), _full_spec((MLA_Q_LORA, qk)),
                  _full_spec((1, MLA_KV_LORA)), _full_spec((MLA_KV_LORA, BRANCH)),
                  _full_spec((MLA_KV_LORA, BRANCH)), rope_spec, rope_spec],
        out_specs=[_tok_spec(tm, qk), _tok_spec(tm, qk), _tok_spec(tm, BRANCH), _tok_spec(tm, BRANCH)],
        out_shape=[jax.ShapeDtypeStruct((b, s, qk), BF16), jax.ShapeDtypeStruct((b, s, qk), BF16),
                   jax.ShapeDtypeStruct((b, s, BRANCH), BF16), jax.ShapeDtypeStruct((b, s, BRANCH), BF16)],
        compiler_params=_cparams(("parallel", "parallel")),
        name="mla_in",
    )(x, g, w1, gq, wuq, gkv, wuk, wuv, rq, rk)


def _mla_attn_kernel(q_ref, k_ref, v_ref, z_ref, o_ref, *, tk, nk):
    q = q_ref[...]
    tq = q.shape[0]

    def body(j, carry):
        m, l, acc = carry
        k0 = pl.multiple_of(j * tk, tk)
        sc = _dot_nt(q, k_ref[pl.ds(k0, tk), :])
        m_new = jnp.maximum(m, jnp.max(sc, axis=-1, keepdims=True))
        alpha = jnp.exp(m - m_new)
        p = jnp.exp(sc - m_new)
        l = alpha * l + jnp.sum(p, axis=-1, keepdims=True)
        acc = alpha * acc + _dot(p.astype(BF16), v_ref[pl.ds(k0, tk), :])
        return m_new, l, acc

    init = (jnp.full((tq, 1), -jnp.inf, F32), jnp.zeros((tq, 1), F32), jnp.zeros((tq, MLA_V), F32))
    m, l, acc = lax.fori_loop(0, nk, body, init)
    o = acc * (1.0 / l)
    o_ref[...] = (o * _silu(z_ref[...].astype(F32))).astype(BF16)


def _mla_attn(q, k, v, z, tq, tk):
    b, s, _ = v.shape
    return pl.pallas_call(
        functools.partial(_mla_attn_kernel, tk=tk, nk=s // tk),
        grid=(b, MLA_HEADS, s // tq),
        in_specs=[pl.BlockSpec((None, tq, MLA_QK), lambda bb, h, i: (bb, i, h)),
                  pl.BlockSpec((None, s, MLA_QK), lambda bb, h, i: (bb, 0, h)),
                  pl.BlockSpec((None, s, MLA_V), lambda bb, h, i: (bb, 0, h)),
                  pl.BlockSpec((None, tq, MLA_V), lambda bb, h, i: (bb, i, h))],
        out_specs=pl.BlockSpec((None, tq, MLA_V), lambda bb, h, i: (bb, i, h)),
        out_shape=jax.ShapeDtypeStruct((b, s, BRANCH), BF16),
        compiler_params=_cparams(("parallel", "parallel", "arbitrary")),
        name="mla_attn",
    )(q, k, v, z)


def _hg_in_kernel(x_ref, g_ref, w_ref, lbr_ref, q_ref, lf_ref, lb_ref, i_ref, z_ref, *, li):
    h = _rmsnorm(x_ref[...], g_ref[...]).astype(BF16)
    raw = lbr_ref[...]
    e = jnp.exp(raw - jnp.max(raw, axis=0, keepdims=True))
    sm = e / jnp.sum(e, axis=0, keepdims=True)
    lower = jnp.zeros_like(sm[0])
    for dd in range(1, li + 1):
        lower = lower + sm[dd]
    q_ref[...] = _silu(_dot(h, w_ref[:, :BRANCH])).astype(BF16)
    for idx, ref in ((0, lf_ref), (1, lb_ref)):
        lb = lower[idx:idx + 1]
        raw_f = _dot(h, w_ref[:, (1 + idx) * BRANCH:(2 + idx) * BRANCH])
        ref[...] = jnp.log(lb + (1.0 - lb) * _sigmoid(raw_f))
    i_ref[...] = _dot(h, w_ref[:, 3 * BRANCH:4 * BRANCH]).astype(BF16)
    z_ref[...] = _dot(h, w_ref[:, 4 * BRANCH:]).astype(BF16)


def _hg_in(x, g, w, lb_raw, li, tm):
    b, s, d = x.shape
    bf = jax.ShapeDtypeStruct((b, s, BRANCH), BF16)
    f32 = jax.ShapeDtypeStruct((b, s, BRANCH), F32)
    return pl.pallas_call(
        functools.partial(_hg_in_kernel, li=li),
        grid=(b, s // tm),
        in_specs=[_tok_spec(tm, d), _full_spec((1, d)), _full_spec((d, 5 * BRANCH)),
                  _full_spec((DEPTH, 2, BRANCH))],
        out_specs=[_tok_spec(tm, BRANCH)] * 5,
        out_shape=[bf, f32, f32, bf, bf],
        compiler_params=_cparams(("parallel", "parallel")),
        name="hg_in",
    )(x, g, w, lb_raw)


def _hg_chunk(q, lf, v, st, tri3, masks, eye, fwd):
    L = HG_CHUNK
    hi = lf.astype(BF16)
    r1 = lf - hi.astype(F32)
    mid = r1.astype(BF16)
    lo = (r1 - mid.astype(F32)).astype(BF16)
    cum = _dot(tri3, jnp.concatenate([hi, mid, lo], axis=0))
    k = 1.0 - jnp.exp(lf)
    row = lax.broadcasted_iota(jnp.int32, (L, HG_DK), 0)
    a = jnp.where(eye, jnp.sum(q * k, axis=-1, keepdims=True), 0.0)
    for n, same_block in zip(HG_LEVELS, masks):
        half = n // 2
        upper = (row & (n - 1)) >= half
        q_side = upper if fwd else jnp.logical_not(upper)
        if n == 2:
            gl = jnp.where(q_side, lf, 0.0)
        else:
            ref_row = half - 1 if fwd else half
            c3 = cum.reshape(L // n, n, HG_DK)
            cm = jnp.broadcast_to(c3[:, ref_row:ref_row + 1, :], c3.shape).reshape(L, HG_DK)
            gl = -jnp.abs(cum - cm)
        e = jnp.exp(gl)
        qt = jnp.where(q_side, q * e, 0.0).astype(BF16)
        kt = jnp.where(q_side, 0.0, k * e).astype(BF16)
        a = a + jnp.where(same_block, _dot_nt(qt, kt), 0.0)
    edge = cum[L - 1:L] if fwd else cum[0:1]
    o = _dot(a.astype(BF16), v) + _dot_nt((q * jnp.exp(cum)).astype(BF16), st.astype(BF16))
    kh = (k * jnp.exp(edge - cum)).astype(BF16)
    st_new = st * jnp.exp(edge) + _dot_tn(v, kh)
    return o, st_new


def _hg_scan_kernel(q_ref, lf_ref, lb_ref, v_ref, z_ref, g_ref, o_ref,
                    of_ref, ob_ref, sf_ref, sb_ref, *, nchunks, fin_rows):
    L = HG_CHUNK
    t_i = lax.broadcasted_iota(jnp.int32, (L, L), 0)
    s_i = lax.broadcasted_iota(jnp.int32, (L, L), 1)
    eye = t_i == s_i
    masks = [(t_i ^ s_i) < n for n in HG_LEVELS]
    t3 = lax.broadcasted_iota(jnp.int32, (L, 3 * L), 0)
    u3 = lax.broadcasted_iota(jnp.int32, (L, 3 * L), 1) & (L - 1)
    tri_f = (u3 <= t3).astype(BF16)
    tri_b = (u3 >= t3).astype(BF16)
    sf_ref[...] = jnp.zeros_like(sf_ref)
    sb_ref[...] = jnp.zeros_like(sb_ref)

    def body(i, carry):
        for fwd, gate_ref, st_ref, out_ref, c in ((True, lf_ref, sf_ref, of_ref, i),
                                                  (False, lb_ref, sb_ref, ob_ref, nchunks - 1 - i)):
            r0 = pl.multiple_of(c * L, L)
            rows = pl.ds(r0, L)
            o, st_new = _hg_chunk(q_ref[rows, :].astype(F32), gate_ref[rows, :], v_ref[rows, :],
                                  st_ref[...], tri_f if fwd else tri_b, masks, eye, fwd)
            out_ref[rows, :] = o
            st_ref[...] = st_new
        return carry

    lax.fori_loop(0, nchunks, body, 0)

    def fin(i, carry):
        rows = pl.ds(pl.multiple_of(i * fin_rows, fin_rows), fin_rows)
        o = of_ref[rows, :] + ob_ref[rows, :]
        y = _rmsnorm(o, g_ref[...])
        o_ref[rows, :] = (y * _silu(z_ref[rows, :].astype(F32))).astype(BF16)
        return carry

    lax.fori_loop(0, (nchunks * L) // fin_rows, fin, 0)


def _hg_scan(q, lf, lb, v, z, g_out):
    b, s, _ = q.shape
    head = pl.BlockSpec((None, s, HG_DK), lambda bb, h: (bb, 0, h))
    fin_rows = min(256, s)
    return pl.pallas_call(
        functools.partial(_hg_scan_kernel, nchunks=s // HG_CHUNK, fin_rows=fin_rows),
        grid=(b, HG_HEADS),
        in_specs=[head] * 5 + [pl.BlockSpec((1, HG_DK), lambda bb, h: (0, h))],
        out_specs=head,
        out_shape=jax.ShapeDtypeStruct((b, s, BRANCH), BF16),
        scratch_shapes=[pltpu.VMEM((s, HG_DK), F32), pltpu.VMEM((s, HG_DK), F32),
                        pltpu.VMEM((HG_DK, HG_DK), F32), pltpu.VMEM((HG_DK, HG_DK), F32)],
        compiler_params=_cparams(("parallel", "parallel")),
        name="hg_scan",
    )(q, lf, lb, v, z, g_out)


def _rot_half_cols(w):
    half = MLA_ROPE // 2
    return jnp.concatenate([-w[..., half:], w[..., :half]], axis=-1)


def _prep(s, norm_g, fn_w_in, fn_w_mix, fn_w_out, na_w_in, na_rpb, na_w_out,
          mla_w_in, mla_g_q, mla_w_uq, mla_g_kv, mla_w_ukv, mla_w_out,
          hg_w_in, hg_lb_raw, hg_g_out, hg_w_out, ple_w, ple_gate_w, final_g, tk):
    bf = lambda t: t.astype(BF16)
    row = lambda t: t.reshape(1, -1).astype(F32)
    w = {}
    w["norm_g"] = [row(norm_g[i]) for i in range(DEPTH)]
    w["final_g"] = row(final_g)
    w["ple_w"] = [bf(ple_w[i]) for i in range(DEPTH)]
    w["gate_w"] = [bf(ple_gate_w[i]) for i in range(DEPTH)]
    w["w_out"] = [bf(fn_w_out[0]), bf(na_w_out[0]), bf(mla_w_out[0]), bf(hg_w_out[0])]
    w["fn_w_in"] = bf(fn_w_in[0])
    w["fn_w2"] = _fn_fold(fn_w_mix[0].astype(F32), s)
    w["dft"] = _dft_mats(s, tk)
    w["na_w_in"] = bf(na_w_in[0])
    w["na_wkt"] = bf(na_w_in[0][:, BRANCH:2 * BRANCH].T)
    w["na_table"] = _na_bias_table(na_rpb[0].astype(F32))
    wi = mla_w_in[0]
    c0, c1, c2 = MLA_Q_LORA, MLA_Q_LORA + MLA_KV_LORA, MLA_Q_LORA + MLA_KV_LORA + MLA_ROPE
    wkpe = wi[:, c1:c2]
    wrot = _rot_half_cols(wkpe)
    w["mla_w1"] = bf(jnp.concatenate([wi[:, :c1], wkpe, wkpe, wrot, wrot, wi[:, c2:]], axis=1))
    wq = mla_w_uq[0].reshape(MLA_Q_LORA, MLA_HEADS, MLA_NOPE + MLA_ROPE)
    wq_pe = wq[..., MLA_NOPE:]
    w["mla_wuq"] = bf(jnp.concatenate([wq[..., :MLA_NOPE], wq_pe, _rot_half_cols(wq_pe)], axis=-1)
                      .reshape(MLA_Q_LORA, MLA_HEADS * MLA_QK))
    wkv = mla_w_ukv[0].reshape(MLA_KV_LORA, MLA_HEADS, MLA_NOPE + MLA_V)
    w["mla_wuk"] = bf(wkv[..., :MLA_NOPE].reshape(MLA_KV_LORA, BRANCH))
    w["mla_wuv"] = bf(wkv[..., MLA_NOPE:].reshape(MLA_KV_LORA, BRANCH))
    w["mla_gq"] = row(mla_g_q[0])
    w["mla_gkv"] = row(mla_g_kv[0])
    half = MLA_ROPE // 2
    inv = ROPE_THETA ** (-jnp.arange(half, dtype=F32) / half)
    ang = jnp.arange(s, dtype=F32)[:, None] * inv[None, :]
    cos, sin = jnp.cos(ang), jnp.sin(ang)
    cos64 = jnp.concatenate([cos, cos], axis=-1)
    sin64 = jnp.concatenate([sin, sin], axis=-1)
    scale = (MLA_NOPE + MLA_ROPE) ** -0.5
    w["mla_rq"] = scale * jnp.concatenate([jnp.ones((s, MLA_NOPE), F32), cos64, sin64], axis=-1)
    w["mla_rk"] = jnp.concatenate([cos64, cos64, sin64, sin64], axis=-1)
    w["hg_w_in"] = bf(hg_w_in[0])
    w["hg_lb_raw"] = hg_lb_raw.astype(F32)
    w["hg_g_out"] = row(hg_g_out[0])
    return w


def _trunk(x, p, w, tm, tk, tq, tkv):
    def finish(x, o, li, final=False):
        return _out_proj(x, o, p, li, w["w_out"][li], w["gate_w"][li], w["ple_w"][li],
                         w["final_g"] if final else None, tm)

    u, z = _fn_in(x, w["norm_g"][0], w["fn_w_in"], tm)
    o = _fn_dft(w["dft"], u, z, w["fn_w2"], tk)
    x = finish(x, o, 0)
    q, kt, v, z = _na_in(x, w["norm_g"][1], w["na_w_in"], w["na_wkt"], tm)
    kt_shift = jnp.concatenate([kt[:, :, GRID_W:], jnp.zeros_like(kt[:, :, :GRID_W])], axis=2)
    o = _na_attn(q, jnp.stack([kt, kt_shift], axis=1), v, z, w["na_table"])
    x = finish(x, o, 1)
    q, k, v, z = _mla_in(x, w["norm_g"][2], w["mla_w1"], w["mla_gq"], w["mla_wuq"], w["mla_gkv"],
                         w["mla_wuk"], w["mla_wuv"], w["mla_rq"], w["mla_rk"], tm)
    o = _mla_attn(q, k, v, z, tq, tkv)
    x = finish(x, o, 2)
    q, lf, lb, v, z = _hg_in(x, w["norm_g"][3], w["hg_w_in"], w["hg_lb_raw"], 3, tm)
    o = _hg_scan(q, lf, lb, v, z, w["hg_g_out"])
    return finish(x, o, 3, final=True)


def kernel(x_prompt, x_sample, p_prompt, p_sample, norm_g, fn_w_in, fn_w_mix, fn_w_out, na_w_in, na_rpb, na_w_out, mla_w_in, mla_g_q, mla_w_uq, mla_g_kv, mla_w_ukv, mla_w_out, hg_w_in, hg_lb_raw, hg_g_out, hg_w_out, ple_w, ple_gate_w, final_g):
    s = x_prompt.shape[1]
    assert x_sample.shape[1] == s
    tm = min(512, s)
    tk = min(512, s)
    tq = min(256, s)
    tkv = min(512, s)
    w = _prep(s, norm_g, fn_w_in, fn_w_mix, fn_w_out, na_w_in, na_rpb, na_w_out,
              mla_w_in, mla_g_q, mla_w_uq, mla_g_kv, mla_w_ukv, mla_w_out,
              hg_w_in, hg_lb_raw, hg_g_out, hg_w_out, ple_w, ple_gate_w, final_g, tk)
    y_prompt = _trunk(x_prompt, p_prompt, w, tm, tk, tq, tkv)
    y_sample = _trunk(x_sample, p_sample, w, tm, tk, tq, tkv)
    return (y_prompt, y_sample)
```

```python
import functools
import math

import numpy as np
import jax
import jax.numpy as jnp
from jax import lax
from jax.experimental import pallas as pl
from jax.experimental.pallas import tpu as pltpu

F32 = jnp.float32
BF16 = jnp.bfloat16

D_MODEL = 1024
DEPTH = 4
BRANCH = 1024
PLE_DIM = 256
GRID_W = 64
EPS = 1e-6
LANE = 128
VMEM_LIMIT = 56 * 1024 * 1024

FN_GROUPS = 8
FN_GDIM = BRANCH // FN_GROUPS

NA_HEADS = 32
NA_HDIM = BRANCH // NA_HEADS
NA_KH = 8
NA_KW = 16
NA_HPG = LANE // NA_HDIM
NA_GROUPS = NA_HEADS // NA_HPG
NA_NEG = -1e30

MLA_HEADS = 8
MLA_NOPE = 128
MLA_ROPE = 64
MLA_V = BRANCH // MLA_HEADS
MLA_Q_LORA = 384
MLA_KV_LORA = 256
MLA_QK = MLA_NOPE + 2 * MLA_ROPE
ROPE_THETA = 10000.0

HG_HEADS = 8
HG_DK = BRANCH // HG_HEADS
HG_CHUNK = 64
HG_LEVELS = (64, 32, 16, 8, 4, 2)


def _cparams(sem):
    return pltpu.CompilerParams(dimension_semantics=sem, vmem_limit_bytes=VMEM_LIMIT)


def _rmsnorm(x, g):
    ms = jnp.mean(x * x, axis=-1, keepdims=True)
    return x * lax.rsqrt(ms + EPS) * g


def _sigmoid(x):
    return 1.0 / (1.0 + jnp.exp(-x))


def _silu(x):
    return x * _sigmoid(x)


def _dot(a, b):
    return jnp.dot(a, b, preferred_element_type=F32)


def _dot_nt(a, b):
    return lax.dot_general(a, b, (((1,), (1,)), ((), ())), preferred_element_type=F32)


def _dot_tn(a, b):
    return lax.dot_general(a, b, (((0,), (0,)), ((), ())), preferred_element_type=F32)


def _tok_spec(tm, width):
    return pl.BlockSpec((None, tm, width), lambda b, i: (b, i, 0))


def _full_spec(shape):
    nd = len(shape)
    return pl.BlockSpec(shape, lambda *_: (0,) * nd)


def _fn_in_kernel(x_ref, g_ref, w_ref, u_ref, z_ref):
    h = _rmsnorm(x_ref[...], g_ref[...]).astype(BF16)
    u_ref[...] = _dot(h, w_ref[:, :BRANCH]).astype(BF16)
    z_ref[...] = _dot(h, w_ref[:, BRANCH:]).astype(BF16)


def _fn_in(x, g, w, tm):
    b, s, d = x.shape
    return pl.pallas_call(
        _fn_in_kernel,
        grid=(b, s // tm),
        in_specs=[_tok_spec(tm, d), _full_spec((1, d)), _full_spec((d, 2 * BRANCH))],
        out_specs=[_tok_spec(tm, BRANCH)] * 2,
        out_shape=[jax.ShapeDtypeStruct((b, s, BRANCH), BF16)] * 2,
        compiler_params=_cparams(("parallel", "parallel")),
        name="fn_in",
    )(x, g, w)


def _fn_fold_kernel(c_ref, s_ref, w_ref, o_ref, *, scale):
    w = w_ref[...]
    hi = lax.Precision.HIGHEST
    wc = jnp.dot(c_ref[...], w, precision=hi, preferred_element_type=F32)
    ws = jnp.dot(s_ref[...], w, precision=hi, preferred_element_type=F32)
    o_ref[:FN_GDIM, :] = (scale * wc).astype(BF16)
    o_ref[FN_GDIM:, :] = (-scale * ws).astype(BF16)


def _fn_fold(w_mix, s):
    c = jnp.arange(FN_GDIM, dtype=jnp.int32)
    ang = ((c[:, None] * c[None, :]) % FN_GDIM).astype(F32) * (2.0 * math.pi / FN_GDIM)
    scale = 1.0 / math.sqrt(s * FN_GDIM)
    gspec = pl.BlockSpec((None, FN_GDIM, FN_GDIM), lambda g: (g, 0, 0))
    return pl.pallas_call(
        functools.partial(_fn_fold_kernel, scale=scale),
        grid=(FN_GROUPS,),
        in_specs=[_full_spec((FN_GDIM, FN_GDIM))] * 2 + [gspec],
        out_specs=pl.BlockSpec((None, 2 * FN_GDIM, FN_GDIM), lambda g: (g, 0, 0)),
        out_shape=jax.ShapeDtypeStruct((FN_GROUPS, 2 * FN_GDIM, FN_GDIM), BF16),
        name="fn_fold",
    )(jnp.cos(ang), jnp.sin(ang), w_mix)


def _dft_mats(s, tk):
    k = jnp.arange(s, dtype=jnp.int32)
    ang = ((k[:, None] * k[None, :]) % s).astype(F32) * (2.0 * math.pi / s)
    c = jnp.cos(ang).astype(BF16).reshape(s // tk, tk, s)
    sn = jnp.sin(ang).astype(BF16).reshape(s // tk, tk, s)
    return jnp.concatenate([c, sn], axis=1)


def _fn_dft_kernel(m_ref, u_ref, z_ref, w2_ref, o_ref, *, tk):
    r = _dot(m_ref[...], u_ref[...])
    a = r[:tk].astype(BF16)
    bm = r[tk:].astype(BF16)
    for g in range(FN_GROUPS):
        sl = slice(g * FN_GDIM, (g + 1) * FN_GDIM)
        ab = jnp.concatenate([a[:, sl], bm[:, sl]], axis=1)
        y = _dot(ab, w2_ref[g])
        o_ref[:, sl] = (y * _silu(z_ref[:, sl].astype(F32))).astype(BF16)


def _fn_dft(mats, u, z, w2, tk):
    b, s, _ = u.shape
    return pl.pallas_call(
        functools.partial(_fn_dft_kernel, tk=tk),
        grid=(b, s // tk),
        in_specs=[pl.BlockSpec((None, 2 * tk, s), lambda bb, j: (j, 0, 0)),
                  pl.BlockSpec((None, s, BRANCH), lambda bb, j: (bb, 0, 0)),
                  _tok_spec(tk, BRANCH),
                  _full_spec((FN_GROUPS, 2 * FN_GDIM, FN_GDIM))],
        out_specs=_tok_spec(tk, BRANCH),
        out_shape=jax.ShapeDtypeStruct((b, s, BRANCH), BF16),
        compiler_params=_cparams(("parallel", "arbitrary")),
        name="fn_dft",
    )(mats, u, z, w2)


def _out_kernel(x_ref, o_ref, p_ref, wo_ref, gw_ref, pw_ref, *rest, final):
    y_ref = rest[-1]
    x1 = x_ref[...] + _dot(o_ref[...], wo_ref[...])
    gate = _sigmoid(_dot(x1.astype(BF16), gw_ref[...]))
    x2 = x1 + gate * _dot(p_ref[...].astype(BF16), pw_ref[...])
    if final:
        x2 = _rmsnorm(x2, rest[0][...])
    y_ref[...] = x2


def _out_proj(x, o, p, li, wo, gw, pw, final_g, tm):
    b, s, d = x.shape
    final = final_g is not None
    in_specs = [_tok_spec(tm, d), _tok_spec(tm, BRANCH),
                pl.BlockSpec((None, None, tm, PLE_DIM), lambda bb, i: (li, bb, i, 0)),
                _full_spec((BRANCH, d)), _full_spec((d, d)), _full_spec((PLE_DIM, d))]
    args = [x, o, p, wo, gw, pw]
    if final:
        in_specs.append(_full_spec((1, d)))
        args.append(final_g)
    return pl.pallas_call(
        functools.partial(_out_kernel, final=final),
        grid=(b, s // tm),
        in_specs=in_specs,
        out_specs=_tok_spec(tm, d),
        out_shape=jax.ShapeDtypeStruct((b, s, d), F32),
        compiler_params=_cparams(("parallel", "parallel")),
        name="out_proj",
    )(*args)


def _na_in_kernel(x_ref, g_ref, w_ref, wkt_ref, q_ref, kt_ref, v_ref, z_ref):
    h = _rmsnorm(x_ref[...], g_ref[...]).astype(BF16)
    q_ref[...] = (_dot(h, w_ref[:, :BRANCH]) * (NA_HDIM ** -0.5)).astype(BF16)
    kt_ref[...] = _dot_nt(wkt_ref[...], h).astype(BF16)
    v_ref[...] = _dot(h, w_ref[:, 2 * BRANCH:3 * BRANCH]).astype(BF16)
    z_ref[...] = _dot(h, w_ref[:, 3 * BRANCH:]).astype(BF16)


def _na_in(x, g, w, wkt, tm):
    b, s, d = x.shape
    tok = jax.ShapeDtypeStruct((b, s, BRANCH), BF16)
    return pl.pallas_call(
        _na_in_kernel,
        grid=(b, s // tm),
        in_specs=[_tok_spec(tm, d), _full_spec((1, d)), _full_spec((d, 4 * BRANCH)),
                  _full_spec((BRANCH, d))],
        out_specs=[_tok_spec(tm, BRANCH),
                   pl.BlockSpec((None, BRANCH, tm), lambda bb, i: (bb, 0, i)),
                   _tok_spec(tm, BRANCH), _tok_spec(tm, BRANCH)],
        out_shape=[tok, jax.ShapeDtypeStruct((b, BRANCH, s), BF16), tok, tok],
        compiler_params=_cparams(("parallel", "parallel")),
        name="na_in",
    )(x, g, w, wkt)


def _na_bias_table(rpb):
    d = np.arange(NA_KH)
    i = np.arange(NA_KH)
    ro = i[None, :] - d[:, None] + (NA_KH - 1)
    c = np.arange(GRID_W)
    kc = np.arange(GRID_W)
    win = np.clip(c - NA_KW // 2, 0, GRID_W - NA_KW)
    valid = (kc[None, :] >= win[:, None]) & (kc[None, :] < win[:, None] + NA_KW)
    pad = GRID_W - NA_KW
    r1 = jnp.pad(rpb[:, ro, :], ((0, 0), (0, 0), (0, 0), (pad, pad)))
    cols = [r1[..., GRID_W - 1 - cc:2 * GRID_W - 1 - cc] for cc in range(GRID_W)]
    bias = jnp.stack(cols, axis=2)
    bias = jnp.where(jnp.asarray(valid)[None, None, :, None, :], bias, NA_NEG)
    return bias.reshape(NA_GROUPS, NA_HPG, NA_KH, GRID_W, NA_KH * GRID_W).astype(F32)


def _na_attn_kernel(q_ref, kt_ref, v_ref, z_ref, t_ref, o_ref, *, rows):
    lane = lax.broadcasted_iota(jnp.int32, (GRID_W, LANE), 1)
    head_masks = [(lane >= hh * NA_HDIM) & (lane < (hh + 1) * NA_HDIM) for hh in range(NA_HPG)]
    nkeys = NA_KH * GRID_W

    def body(r, carry):
        rs = jnp.clip(r - NA_KH // 2, 0, rows - NA_KH)
        d = r - rs
        par = rs & 1
        q0 = pl.multiple_of(r * GRID_W, GRID_W)
        q_r = q_ref[pl.ds(q0, GRID_W), :]
        zero = jnp.zeros_like(q_r)
        qs = jnp.concatenate([jnp.where(m, q_r, zero) for m in head_masks], axis=0)
        k0 = pl.multiple_of((rs - par) * GRID_W, LANE)
        kwin = kt_ref[par, :, pl.ds(k0, nkeys)]
        bias = jnp.concatenate([t_ref[hh, d] for hh in range(NA_HPG)], axis=0)
        sc = _dot(qs, kwin) + bias
        m = jnp.max(sc, axis=-1, keepdims=True)
        p = jnp.exp(sc - m)
        l = jnp.sum(p, axis=-1, keepdims=True)
        v0 = pl.multiple_of(rs * GRID_W, GRID_W)
        vwin = v_ref[pl.ds(v0, nkeys), :]
        o = _dot(p.astype(BF16), vwin) * (1.0 / l)
        acc = jnp.zeros((GRID_W, LANE), F32)
        for hh in range(NA_HPG):
            acc = jnp.where(head_masks[hh], o[hh * GRID_W:(hh + 1) * GRID_W], acc)
        zz = z_ref[pl.ds(q0, GRID_W), :].astype(F32)
        o_ref[pl.ds(q0, GRID_W), :] = (acc * _silu(zz)).astype(BF16)
        return carry

    lax.fori_loop(0, rows, body, 0)


def _na_attn(q, kt2, v, z, table):
    b, s, _ = q.shape
    rows = s // GRID_W
    assert rows >= NA_KH
    grp = pl.BlockSpec((None, s, LANE), lambda g, bb: (bb, 0, g))
    return pl.pallas_call(
        functools.partial(_na_attn_kernel, rows=rows),
        grid=(NA_GROUPS, b),
        in_specs=[grp,
                  pl.BlockSpec((None, 2, LANE, s), lambda g, bb: (bb, 0, g, 0)),
                  grp, grp,
                  pl.BlockSpec((None, NA_HPG, NA_KH, GRID_W, NA_KH * GRID_W),
                               lambda g, bb: (g, 0, 0, 0, 0))],
        out_specs=grp,
        out_shape=jax.ShapeDtypeStruct((b, s, BRANCH), BF16),
        compiler_params=_cparams(("parallel", "parallel")),
        name="na_attn",
    )(q, kt2, v, z, table)


def _mla_in_kernel(x_ref, g_ref, w1_ref, gq_ref, wuq_ref, gkv_ref, wuk_ref, wuvt_ref,
                   rq_ref, rk_ref, q_ref, k_ref, vt_ref, z_ref):
    h = _rmsnorm(x_ref[...], g_ref[...]).astype(BF16)
    c0, c1, c2, c3 = MLA_Q_LORA, MLA_Q_LORA + MLA_KV_LORA, MLA_Q_LORA + MLA_KV_LORA + LANE, \
        MLA_Q_LORA + MLA_KV_LORA + 2 * LANE
    cq = _dot(h, w1_ref[:, :c0])
    ckv = _dot(h, w1_ref[:, c0:c1])
    kp = _dot(h, w1_ref[:, c1:c2])
    kr = _dot(h, w1_ref[:, c2:c3])
    z_ref[...] = _dot(h, w1_ref[:, c3:]).astype(BF16)
    cqn = _rmsnorm(cq, gq_ref[...]).astype(BF16)
    ckvn = _rmsnorm(ckv, gkv_ref[...]).astype(BF16)
    rq = rq_ref[...]
    for hh in range(MLA_HEADS):
        sl = slice(hh * MLA_QK, (hh + 1) * MLA_QK)
        q_ref[:, sl] = (_dot(cqn, wuq_ref[:, sl]) * rq).astype(BF16)
    krr = (kp * rk_ref[:, :LANE] + kr * rk_ref[:, LANE:]).astype(BF16)
    kn = _dot(ckvn, wuk_ref[...]).astype(BF16)
    for hh in range(MLA_HEADS):
        k_ref[:, hh * MLA_QK:hh * MLA_QK + MLA_NOPE] = kn[:, hh * MLA_NOPE:(hh + 1) * MLA_NOPE]
        k_ref[:, hh * MLA_QK + MLA_NOPE:(hh + 1) * MLA_QK] = krr
    vt_ref[...] = _dot_nt(wuvt_ref[...], ckvn).astype(BF16)


def _mla_in(x, g, w1, gq, wuq, gkv, wuk, wuvt, rq, rk, tm):
    b, s, d = x.shape
    n1 = w1.shape[1]
    qk = MLA_HEADS * MLA_QK
    rope_spec = pl.BlockSpec((tm, 2 * LANE), lambda bb, i: (i, 0))
    return pl.pallas_call(
        _mla_in_kernel,
        grid=(b, s // tm),
        in_specs=[_tok_spec(tm, d), _full_spec((1, d)), _full_spec((d, n1)),
                  _full_spec((1, MLA_Q_LORA)), _full_spec((MLA_Q_LORA, qk)),
                  _full_spec((1, MLA_KV_LORA)), _full_spec((MLA_KV_LORA, BRANCH)),
                  _full_spec((BRANCH, MLA_KV_LORA)), rope_spec, rope_spec],
        out_specs=[_tok_spec(tm, qk), _tok_spec(tm, qk),
                   pl.BlockSpec((None, BRANCH, tm), lambda bb, i: (bb, 0, i)), _tok_spec(tm, BRANCH)],
        out_shape=[jax.ShapeDtypeStruct((b, s, qk), BF16), jax.ShapeDtypeStruct((b, s, qk), BF16),
                   jax.ShapeDtypeStruct((b, BRANCH, s), BF16), jax.ShapeDtypeStruct((b, s, BRANCH), BF16)],
        compiler_params=_cparams(("parallel", "parallel")),
        name="mla_in",
    )(x, g, w1, gq, wuq, gkv, wuk, wuvt, rq, rk)


def _mla_attn_kernel(q_ref, k_ref, vt_ref, z_ref, o_ref, *, tk, nk):
    q = q_ref[...]
    tq = q.shape[0]
    m = jnp.full((1, tq), -jnp.inf, F32)
    l = jnp.zeros((1, tq), F32)
    acc = jnp.zeros((MLA_V, tq), F32)
    for j in range(nk):
        st = _dot_nt(k_ref[j * tk:(j + 1) * tk, :], q)
        m_new = jnp.maximum(m, jnp.max(st, axis=0, keepdims=True))
        alpha = jnp.exp2(m - m_new)
        p = jnp.exp2(st - m_new)
        l = alpha * l + jnp.sum(p, axis=0, keepdims=True)
        acc = alpha * acc + _dot(vt_ref[:, j * tk:(j + 1) * tk], p.astype(BF16))
        m = m_new
    o = (acc * (1.0 / l)).T
    o_ref[...] = (o * _silu(z_ref[...].astype(F32))).astype(BF16)


def _mla_attn(q, k, vt, z, tq, tk):
    b, _, s = vt.shape
    return pl.pallas_call(
        functools.partial(_mla_attn_kernel, tk=tk, nk=s // tk),
        grid=(b, MLA_HEADS, s // tq),
        in_specs=[pl.BlockSpec((None, tq, MLA_QK), lambda bb, h, i: (bb, i, h)),
                  pl.BlockSpec((None, s, MLA_QK), lambda bb, h, i: (bb, 0, h)),
                  pl.BlockSpec((None, MLA_V, s), lambda bb, h, i: (bb, h, 0)),
                  pl.BlockSpec((None, tq, MLA_V), lambda bb, h, i: (bb, i, h))],
        out_specs=pl.BlockSpec((None, tq, MLA_V), lambda bb, h, i: (bb, i, h)),
        out_shape=jax.ShapeDtypeStruct((b, s, BRANCH), BF16),
        compiler_params=_cparams(("parallel", "parallel", "arbitrary")),
        name="mla_attn",
    )(q, k, vt, z)


def _hg_in_kernel(x_ref, g_ref, w_ref, lbr_ref, q_ref, lf_ref, lb_ref, i_ref, z_ref, *, li):
    h = _rmsnorm(x_ref[...], g_ref[...]).astype(BF16)
    raw = lbr_ref[...]
    e = jnp.exp(raw - jnp.max(raw, axis=0, keepdims=True))
    sm = e / jnp.sum(e, axis=0, keepdims=True)
    lower = jnp.zeros_like(sm[0])
    for dd in range(1, li + 1):
        lower = lower + sm[dd]
    q_ref[...] = _silu(_dot(h, w_ref[:, :BRANCH])).astype(BF16)
    for idx, ref in ((0, lf_ref), (1, lb_ref)):
        lb = lower[idx:idx + 1]
        raw_f = _dot(h, w_ref[:, (1 + idx) * BRANCH:(2 + idx) * BRANCH])
        ref[...] = jnp.log(lb + (1.0 - lb) * _sigmoid(raw_f))
    i_ref[...] = _dot(h, w_ref[:, 3 * BRANCH:4 * BRANCH]).astype(BF16)
    z_ref[...] = _dot(h, w_ref[:, 4 * BRANCH:]).astype(BF16)


def _hg_in(x, g, w, lb_raw, li, tm):
    b, s, d = x.shape
    bf = jax.ShapeDtypeStruct((b, s, BRANCH), BF16)
    f32 = jax.ShapeDtypeStruct((b, s, BRANCH), F32)
    return pl.pallas_call(
        functools.partial(_hg_in_kernel, li=li),
        grid=(b, s // tm),
        in_specs=[_tok_spec(tm, d), _full_spec((1, d)), _full_spec((d, 5 * BRANCH)),
                  _full_spec((DEPTH, 2, BRANCH))],
        out_specs=[_tok_spec(tm, BRANCH)] * 5,
        out_shape=[bf, f32, f32, bf, bf],
        compiler_params=_cparams(("parallel", "parallel")),
        name="hg_in",
    )(x, g, w, lb_raw)


def _hg_chunk(q, lf, v, st, tri3, masks, eye, fwd):
    L = HG_CHUNK
    hi = lf.astype(BF16)
    r1 = lf - hi.astype(F32)
    mid = r1.astype(BF16)
    lo = (r1 - mid.astype(F32)).astype(BF16)
    cum = _dot(tri3, jnp.concatenate([hi, mid, lo], axis=0))
    k = 1.0 - jnp.exp(lf)
    row = lax.broadcasted_iota(jnp.int32, (L, HG_DK), 0)
    a = jnp.where(eye, jnp.sum(q * k, axis=-1, keepdims=True), 0.0)
    for n, same_block in zip(HG_LEVELS, masks):
        half = n // 2
        upper = (row & (n - 1)) >= half
        q_side = upper if fwd else jnp.logical_not(upper)
        if n == 2:
            gl = jnp.where(q_side, lf, 0.0)
        else:
            ref_row = half - 1 if fwd else half
            c3 = cum.reshape(L // n, n, HG_DK)
            cm = jnp.broadcast_to(c3[:, ref_row:ref_row + 1, :], c3.shape).reshape(L, HG_DK)
            gl = -jnp.abs(cum - cm)
        e = jnp.exp(gl)
        qt = jnp.where(q_side, q * e, 0.0).astype(BF16)
        kt = jnp.where(q_side, 0.0, k * e).astype(BF16)
        a = a + jnp.where(same_block, _dot_nt(qt, kt), 0.0)
    edge = cum[L - 1:L] if fwd else cum[0:1]
    o = _dot(a.astype(BF16), v) + _dot_nt((q * jnp.exp(cum)).astype(BF16), st.astype(BF16))
    kh = (k * jnp.exp(edge - cum)).astype(BF16)
    st_new = st * jnp.exp(edge) + _dot_tn(v, kh)
    return o, st_new


def _hg_scan_kernel(q_ref, lf_ref, lb_ref, v_ref, z_ref, g_ref, o_ref,
                    of_ref, ob_ref, sf_ref, sb_ref, *, nchunks, fin_rows):
    L = HG_CHUNK
    t_i = lax.broadcasted_iota(jnp.int32, (L, L), 0)
    s_i = lax.broadcasted_iota(jnp.int32, (L, L), 1)
    eye = t_i == s_i
    masks = [(t_i ^ s_i) < n for n in HG_LEVELS]
    t3 = lax.broadcasted_iota(jnp.int32, (L, 3 * L), 0)
    u3 = lax.broadcasted_iota(jnp.int32, (L, 3 * L), 1) & (L - 1)
    tri_f = (u3 <= t3).astype(BF16)
    tri_b = (u3 >= t3).astype(BF16)
    sf_ref[...] = jnp.zeros_like(sf_ref)
    sb_ref[...] = jnp.zeros_like(sb_ref)

    def body(i, carry):
        for fwd, gate_ref, st_ref, out_ref, c in ((True, lf_ref, sf_ref, of_ref, i),
                                                  (False, lb_ref, sb_ref, ob_ref, nchunks - 1 - i)):
            r0 = pl.multiple_of(c * L, L)
            rows = pl.ds(r0, L)
            o, st_new = _hg_chunk(q_ref[rows, :].astype(F32), gate_ref[rows, :], v_ref[rows, :],
                                  st_ref[...], tri_f if fwd else tri_b, masks, eye, fwd)
            out_ref[rows, :] = o
            st_ref[...] = st_new
        return carry

    lax.fori_loop(0, nchunks, body, 0)

    def fin(i, carry):
        rows = pl.ds(pl.multiple_of(i * fin_rows, fin_rows), fin_rows)
        o = of_ref[rows, :] + ob_ref[rows, :]
        y = _rmsnorm(o, g_ref[...])
        o_ref[rows, :] = (y * _silu(z_ref[rows, :].astype(F32))).astype(BF16)
        return carry

    lax.fori_loop(0, (nchunks * L) // fin_rows, fin, 0)


def _hg_scan(q, lf, lb, v, z, g_out):
    b, s, _ = q.shape
    head = pl.BlockSpec((None, s, HG_DK), lambda bb, h: (bb, 0, h))
    fin_rows = min(256, s)
    return pl.pallas_call(
        functools.partial(_hg_scan_kernel, nchunks=s // HG_CHUNK, fin_rows=fin_rows),
        grid=(b, HG_HEADS),
        in_specs=[head] * 5 + [pl.BlockSpec((1, HG_DK), lambda bb, h: (0, h))],
        out_specs=head,
        out_shape=jax.ShapeDtypeStruct((b, s, BRANCH), BF16),
        scratch_shapes=[pltpu.VMEM((s, HG_DK), F32), pltpu.VMEM((s, HG_DK), F32),
                        pltpu.VMEM((HG_DK, HG_DK), F32), pltpu.VMEM((HG_DK, HG_DK), F32)],
        compiler_params=_cparams(("parallel", "parallel")),
        name="hg_scan",
    )(q, lf, lb, v, z, g_out)


def _rot_half_cols(w):
    half = MLA_ROPE // 2
    return jnp.concatenate([-w[..., half:], w[..., :half]], axis=-1)


def _prep(s, norm_g, fn_w_in, fn_w_mix, fn_w_out, na_w_in, na_rpb, na_w_out,
          mla_w_in, mla_g_q, mla_w_uq, mla_g_kv, mla_w_ukv, mla_w_out,
          hg_w_in, hg_lb_raw, hg_g_out, hg_w_out, ple_w, ple_gate_w, final_g, tk):
    bf = lambda t: t.astype(BF16)
    row = lambda t: t.reshape(1, -1).astype(F32)
    w = {}
    w["norm_g"] = [row(norm_g[i]) for i in range(DEPTH)]
    w["final_g"] = row(final_g)
    w["ple_w"] = [bf(ple_w[i]) for i in range(DEPTH)]
    w["gate_w"] = [bf(ple_gate_w[i]) for i in range(DEPTH)]
    w["w_out"] = [bf(fn_w_out[0]), bf(na_w_out[0]), bf(mla_w_out[0]), bf(hg_w_out[0])]
    w["fn_w_in"] = bf(fn_w_in[0])
    w["fn_w2"] = _fn_fold(fn_w_mix[0].astype(F32), s)
    w["dft"] = _dft_mats(s, tk)
    w["na_w_in"] = bf(na_w_in[0])
    w["na_wkt"] = bf(na_w_in[0][:, BRANCH:2 * BRANCH].T)
    w["na_table"] = _na_bias_table(na_rpb[0].astype(F32))
    wi = mla_w_in[0]
    c0, c1, c2 = MLA_Q_LORA, MLA_Q_LORA + MLA_KV_LORA, MLA_Q_LORA + MLA_KV_LORA + MLA_ROPE
    wkpe = wi[:, c1:c2]
    wrot = _rot_half_cols(wkpe)
    w["mla_w1"] = bf(jnp.concatenate([wi[:, :c1], wkpe, wkpe, wrot, wrot, wi[:, c2:]], axis=1))
    wq = mla_w_uq[0].reshape(MLA_Q_LORA, MLA_HEADS, MLA_NOPE + MLA_ROPE)
    wq_pe = wq[..., MLA_NOPE:]
    w["mla_wuq"] = bf(jnp.concatenate([wq[..., :MLA_NOPE], wq_pe, _rot_half_cols(wq_pe)], axis=-1)
                      .reshape(MLA_Q_LORA, MLA_HEADS * MLA_QK))
    wkv = mla_w_ukv[0].reshape(MLA_KV_LORA, MLA_HEADS, MLA_NOPE + MLA_V)
    w["mla_wuk"] = bf(wkv[..., :MLA_NOPE].reshape(MLA_KV_LORA, BRANCH))
    w["mla_wuvt"] = bf(wkv[..., MLA_NOPE:].reshape(MLA_KV_LORA, BRANCH).T)
    w["mla_gq"] = row(mla_g_q[0])
    w["mla_gkv"] = row(mla_g_kv[0])
    half = MLA_ROPE // 2
    inv = ROPE_THETA ** (-jnp.arange(half, dtype=F32) / half)
    ang = jnp.arange(s, dtype=F32)[:, None] * inv[None, :]
    cos, sin = jnp.cos(ang), jnp.sin(ang)
    cos64 = jnp.concatenate([cos, cos], axis=-1)
    sin64 = jnp.concatenate([sin, sin], axis=-1)
    scale = (MLA_NOPE + MLA_ROPE) ** -0.5 * math.log2(math.e)
    w["mla_rq"] = scale * jnp.concatenate([jnp.ones((s, MLA_NOPE), F32), cos64, sin64], axis=-1)
    w["mla_rk"] = jnp.concatenate([cos64, cos64, sin64, sin64], axis=-1)
    w["hg_w_in"] = bf(hg_w_in[0])
    w["hg_lb_raw"] = hg_lb_raw.astype(F32)
    w["hg_g_out"] = row(hg_g_out[0])
    return w


def _trunk(x, p, w, tm, tk, tq, tkv):
    def finish(x, o, li, final=False):
        return _out_proj(x, o, p, li, w["w_out"][li], w["gate_w"][li], w["ple_w"][li],
                         w["final_g"] if final else None, tm)

    u, z = _fn_in(x, w["norm_g"][0], w["fn_w_in"], tm)
    o = _fn_dft(w["dft"], u, z, w["fn_w2"], tk)
    x = finish(x, o, 0)
    q, kt, v, z = _na_in(x, w["norm_g"][1], w["na_w_in"], w["na_wkt"], tm)
    kt_shift = jnp.concatenate([kt[:, :, GRID_W:], jnp.zeros_like(kt[:, :, :GRID_W])], axis=2)
    o = _na_attn(q, jnp.stack([kt, kt_shift], axis=1), v, z, w["na_table"])
    x = finish(x, o, 1)
    q, k, vt, z = _mla_in(x, w["norm_g"][2], w["mla_w1"], w["mla_gq"], w["mla_wuq"], w["mla_gkv"],
                          w["mla_wuk"], w["mla_wuvt"], w["mla_rq"], w["mla_rk"], tm)
    o = _mla_attn(q, k, vt, z, tq, tkv)
    x = finish(x, o, 2)
    q, lf, lb, v, z = _hg_in(x, w["norm_g"][3], w["hg_w_in"], w["hg_lb_raw"], 3, tm)
    o = _hg_scan(q, lf, lb, v, z, w["hg_g_out"])
    return finish(x, o, 3, final=True)


def kernel(x_prompt, x_sample, p_prompt, p_sample, norm_g, fn_w_in, fn_w_mix, fn_w_out, na_w_in, na_rpb, na_w_out, mla_w_in, mla_g_q, mla_w_uq, mla_g_kv, mla_w_ukv, mla_w_out, hg_w_in, hg_lb_raw, hg_g_out, hg_w_out, ple_w, ple_gate_w, final_g):
    s = x_prompt.shape[1]
    assert x_sample.shape[1] == s
    tm = min(512, s)
    tk = min(512, s)
    tq = min(512, s)
    tkv = min(512, s)
    w = _prep(s, norm_g, fn_w_in, fn_w_mix, fn_w_out, na_w_in, na_rpb, na_w_out,
              mla_w_in, mla_g_q, mla_w_uq, mla_g_kv, mla_w_ukv, mla_w_out,
              hg_w_in, hg_lb_raw, hg_g_out, hg_w_out, ple_w, ple_gate_w, final_g, tk)
    y_prompt = _trunk(x_prompt, p_prompt, w, tm, tk, tq, tkv)
    y_sample = _trunk(x_sample, p_sample, w, tm, tk, tq, tkv)
    return (y_prompt, y_sample)
```

```python
import functools
import math

import numpy as np
import jax
import jax.numpy as jnp
from jax import lax
from jax.experimental import pallas as pl
from jax.experimental.pallas import tpu as pltpu

F32 = jnp.float32
BF16 = jnp.bfloat16

D_MODEL = 1024
DEPTH = 4
BRANCH = 1024
PLE_DIM = 256
GRID_W = 64
EPS = 1e-6
LANE = 128
VMEM_LIMIT = 56 * 1024 * 1024

FN_GROUPS = 8
FN_GDIM = BRANCH // FN_GROUPS

NA_HEADS = 32
NA_HDIM = BRANCH // NA_HEADS
NA_KH = 8
NA_KW = 16
NA_HPG = LANE // NA_HDIM
NA_GROUPS = NA_HEADS // NA_HPG
NA_NEG = -1e30

MLA_HEADS = 8
MLA_NOPE = 128
MLA_ROPE = 64
MLA_V = BRANCH // MLA_HEADS
MLA_Q_LORA = 384
MLA_KV_LORA = 256
MLA_QK = MLA_NOPE + 2 * MLA_ROPE
ROPE_THETA = 10000.0

HG_HEADS = 8
HG_DK = BRANCH // HG_HEADS
HG_CHUNK = 64
HG_LEVELS = (64, 32, 16, 8, 4, 2)
HG_UNROLL = 4
NA_UNROLL = 4
LOG2E = math.log2(math.e)


def _cparams(sem):
    return pltpu.CompilerParams(dimension_semantics=sem, vmem_limit_bytes=VMEM_LIMIT)


def _rmsnorm(x, g):
    ms = jnp.mean(x * x, axis=-1, keepdims=True)
    return x * lax.rsqrt(ms + EPS) * g


def _sigmoid(x):
    return 1.0 / (1.0 + jnp.exp(-x))


def _silu(x):
    return x * _sigmoid(x)


def _dot(a, b):
    return jnp.dot(a, b, preferred_element_type=F32)


def _dot_nt(a, b):
    return lax.dot_general(a, b, (((1,), (1,)), ((), ())), preferred_element_type=F32)


def _dot_tn(a, b):
    return lax.dot_general(a, b, (((0,), (0,)), ((), ())), preferred_element_type=F32)


def _tok_spec(tm, width):
    return pl.BlockSpec((None, tm, width), lambda b, i: (b, i, 0))


def _full_spec(shape):
    nd = len(shape)
    return pl.BlockSpec(shape, lambda *_: (0,) * nd)


def _fn_in_kernel(x_ref, g_ref, w_ref, u_ref, z_ref):
    h = _rmsnorm(x_ref[...], g_ref[...]).astype(BF16)
    u_ref[...] = _dot(h, w_ref[:, :BRANCH]).astype(BF16)
    z_ref[...] = _dot(h, w_ref[:, BRANCH:]).astype(BF16)


def _fn_in(x, g, w, tm):
    b, s, d = x.shape
    return pl.pallas_call(
        _fn_in_kernel,
        grid=(b, s // tm),
        in_specs=[_tok_spec(tm, d), _full_spec((1, d)), _full_spec((d, 2 * BRANCH))],
        out_specs=[_tok_spec(tm, BRANCH)] * 2,
        out_shape=[jax.ShapeDtypeStruct((b, s, BRANCH), BF16)] * 2,
        compiler_params=_cparams(("parallel", "parallel")),
        name="fn_in",
    )(x, g, w)


def _fn_fold_kernel(c_ref, s_ref, w_ref, o_ref, *, scale):
    w = w_ref[...]
    hi = lax.Precision.HIGHEST
    wc = jnp.dot(c_ref[...], w, precision=hi, preferred_element_type=F32)
    ws = jnp.dot(s_ref[...], w, precision=hi, preferred_element_type=F32)
    o_ref[:FN_GDIM, :] = (scale * wc).astype(BF16)
    o_ref[FN_GDIM:, :] = (-scale * ws).astype(BF16)


def _fn_fold(w_mix, s):
    c = jnp.arange(FN_GDIM, dtype=jnp.int32)
    ang = ((c[:, None] * c[None, :]) % FN_GDIM).astype(F32) * (2.0 * math.pi / FN_GDIM)
    scale = 1.0 / math.sqrt(s * FN_GDIM)
    gspec = pl.BlockSpec((None, FN_GDIM, FN_GDIM), lambda g: (g, 0, 0))
    return pl.pallas_call(
        functools.partial(_fn_fold_kernel, scale=scale),
        grid=(FN_GROUPS,),
        in_specs=[_full_spec((FN_GDIM, FN_GDIM))] * 2 + [gspec],
        out_specs=pl.BlockSpec((None, 2 * FN_GDIM, FN_GDIM), lambda g: (g, 0, 0)),
        out_shape=jax.ShapeDtypeStruct((FN_GROUPS, 2 * FN_GDIM, FN_GDIM), BF16),
        name="fn_fold",
    )(jnp.cos(ang), jnp.sin(ang), w_mix)


def _dft_mats(s, tk):
    k = jnp.arange(s, dtype=jnp.int32)
    ang = ((k[:, None] * k[None, :]) % s).astype(F32) * (2.0 * math.pi / s)
    c = jnp.cos(ang).astype(BF16).reshape(s // tk, tk, s)
    sn = jnp.sin(ang).astype(BF16).reshape(s // tk, tk, s)
    return jnp.concatenate([c, sn], axis=1)


def _fn_dft_kernel(m_ref, u_ref, z_ref, w2_ref, o_ref, *, tk):
    r = _dot(m_ref[...], u_ref[...])
    a = r[:tk].astype(BF16)
    bm = r[tk:].astype(BF16)
    for g in range(FN_GROUPS):
        sl = slice(g * FN_GDIM, (g + 1) * FN_GDIM)
        ab = jnp.concatenate([a[:, sl], bm[:, sl]], axis=1)
        y = _dot(ab, w2_ref[g])
        o_ref[:, sl] = (y * _silu(z_ref[:, sl].astype(F32))).astype(BF16)


def _fn_dft(mats, u, z, w2, tk):
    b, s, _ = u.shape
    return pl.pallas_call(
        functools.partial(_fn_dft_kernel, tk=tk),
        grid=(b, s // tk),
        in_specs=[pl.BlockSpec((None, 2 * tk, s), lambda bb, j: (j, 0, 0)),
                  pl.BlockSpec((None, s, BRANCH), lambda bb, j: (bb, 0, 0)),
                  _tok_spec(tk, BRANCH),
                  _full_spec((FN_GROUPS, 2 * FN_GDIM, FN_GDIM))],
        out_specs=_tok_spec(tk, BRANCH),
        out_shape=jax.ShapeDtypeStruct((b, s, BRANCH), BF16),
        compiler_params=_cparams(("parallel", "arbitrary")),
        name="fn_dft",
    )(mats, u, z, w2)


def _out_kernel(x_ref, o_ref, p_ref, wo_ref, gw_ref, pw_ref, *rest, final):
    y_ref = rest[-1]
    x1 = x_ref[...] + _dot(o_ref[...], wo_ref[...])
    gate = _sigmoid(_dot(x1.astype(BF16), gw_ref[...]))
    x2 = x1 + gate * _dot(p_ref[...].astype(BF16), pw_ref[...])
    if final:
        x2 = _rmsnorm(x2, rest[0][...])
    y_ref[...] = x2


def _out_proj(x, o, p, li, wo, gw, pw, final_g, tm):
    b, s, d = x.shape
    final = final_g is not None
    in_specs = [_tok_spec(tm, d), _tok_spec(tm, BRANCH),
                pl.BlockSpec((None, None, tm, PLE_DIM), lambda bb, i: (li, bb, i, 0)),
                _full_spec((BRANCH, d)), _full_spec((d, d)), _full_spec((PLE_DIM, d))]
    args = [x, o, p, wo, gw, pw]
    if final:
        in_specs.append(_full_spec((1, d)))
        args.append(final_g)
    return pl.pallas_call(
        functools.partial(_out_kernel, final=final),
        grid=(b, s // tm),
        in_specs=in_specs,
        out_specs=_tok_spec(tm, d),
        out_shape=jax.ShapeDtypeStruct((b, s, d), F32),
        compiler_params=_cparams(("parallel", "parallel")),
        name="out_proj",
    )(*args)


def _na_in_kernel(x_ref, g_ref, w_ref, wkt_ref, q_ref, kt_ref, v_ref, z_ref):
    h = _rmsnorm(x_ref[...], g_ref[...]).astype(BF16)
    q_ref[...] = (_dot(h, w_ref[:, :BRANCH]) * (NA_HDIM ** -0.5 * LOG2E)).astype(BF16)
    kt_ref[...] = _dot_nt(wkt_ref[...], h).astype(BF16)
    v_ref[...] = _dot(h, w_ref[:, 2 * BRANCH:3 * BRANCH]).astype(BF16)
    z_ref[...] = _dot(h, w_ref[:, 3 * BRANCH:]).astype(BF16)


def _na_in(x, g, w, wkt, tm):
    b, s, d = x.shape
    tok = jax.ShapeDtypeStruct((b, s, BRANCH), BF16)
    return pl.pallas_call(
        _na_in_kernel,
        grid=(b, s // tm),
        in_specs=[_tok_spec(tm, d), _full_spec((1, d)), _full_spec((d, 4 * BRANCH)),
                  _full_spec((BRANCH, d))],
        out_specs=[_tok_spec(tm, BRANCH),
                   pl.BlockSpec((None, BRANCH, tm), lambda bb, i: (bb, 0, i)),
                   _tok_spec(tm, BRANCH), _tok_spec(tm, BRANCH)],
        out_shape=[tok, jax.ShapeDtypeStruct((b, BRANCH, s), BF16), tok, tok],
        compiler_params=_cparams(("parallel", "parallel")),
        name="na_in",
    )(x, g, w, wkt)


def _na_bias_table(rpb):
    d = np.arange(NA_KH)
    i = np.arange(NA_KH)
    ro = i[None, :] - d[:, None] + (NA_KH - 1)
    c = np.arange(GRID_W)
    kc = np.arange(GRID_W)
    win = np.clip(c - NA_KW // 2, 0, GRID_W - NA_KW)
    valid = (kc[None, :] >= win[:, None]) & (kc[None, :] < win[:, None] + NA_KW)
    pad = GRID_W - NA_KW
    r1 = jnp.pad(rpb[:, ro, :] * LOG2E, ((0, 0), (0, 0), (0, 0), (pad, pad)))
    cols = [r1[..., GRID_W - 1 - cc:2 * GRID_W - 1 - cc] for cc in range(GRID_W)]
    bias = jnp.stack(cols, axis=2)
    bias = jnp.where(jnp.asarray(valid)[None, None, :, None, :], bias, NA_NEG)
    return bias.reshape(NA_GROUPS, NA_HPG, NA_KH, GRID_W, NA_KH * GRID_W).astype(F32)


def _na_attn_kernel(q_ref, kt_ref, v_ref, z_ref, t_ref, o_ref, *, rows):
    lane = lax.broadcasted_iota(jnp.int32, (GRID_W, LANE), 1)
    head_masks = [(lane >= hh * NA_HDIM) & (lane < (hh + 1) * NA_HDIM) for hh in range(NA_HPG)]
    nkeys = NA_KH * GRID_W

    def logits(r):
        rs = jnp.clip(r - NA_KH // 2, 0, rows - NA_KH)
        d = r - rs
        par = rs & 1
        q_r = q_ref[pl.ds(pl.multiple_of(r * GRID_W, GRID_W), GRID_W), :]
        zero = jnp.zeros_like(q_r)
        qs = jnp.concatenate([jnp.where(m, q_r, zero) for m in head_masks], axis=0)
        k0 = pl.multiple_of((rs - par) * GRID_W, LANE)
        kwin = kt_ref[par, :, pl.ds(k0, nkeys)]
        bias = jnp.concatenate([t_ref[hh, d] for hh in range(NA_HPG)], axis=0)
        return _dot(qs, kwin) + bias

    def probs(sc):
        p = jnp.exp2(sc - jnp.max(sc, axis=-1, keepdims=True))
        return p.astype(BF16), jnp.sum(p, axis=-1, keepdims=True)

    def values(r, p, l):
        rs = jnp.clip(r - NA_KH // 2, 0, rows - NA_KH)
        vwin = v_ref[pl.ds(pl.multiple_of(rs * GRID_W, GRID_W), nkeys), :]
        o = _dot(p, vwin) * (1.0 / l)
        acc = jnp.zeros((GRID_W, LANE), F32)
        for hh in range(NA_HPG):
            acc = jnp.where(head_masks[hh], o[hh * GRID_W:(hh + 1) * GRID_W], acc)
        q0 = pl.multiple_of(r * GRID_W, GRID_W)
        zz = z_ref[pl.ds(q0, GRID_W), :].astype(F32)
        o_ref[pl.ds(q0, GRID_W), :] = (acc * _silu(zz)).astype(BF16)

    def body(i, carry):
        rws = [i * NA_UNROLL + u for u in range(NA_UNROLL)]
        scs = [logits(r) for r in rws]
        pls = [probs(sc) for sc in scs]
        for r, (p, l) in zip(rws, pls):
            values(r, p, l)
        return carry

    lax.fori_loop(0, rows // NA_UNROLL, body, 0)


def _na_attn(q, kt2, v, z, table):
    b, s, _ = q.shape
    rows = s // GRID_W
    assert rows >= NA_KH
    grp = pl.BlockSpec((None, s, LANE), lambda g, bb: (bb, 0, g))
    return pl.pallas_call(
        functools.partial(_na_attn_kernel, rows=rows),
        grid=(NA_GROUPS, b),
        in_specs=[grp,
                  pl.BlockSpec((None, 2, LANE, s), lambda g, bb: (bb, 0, g, 0)),
                  grp, grp,
                  pl.BlockSpec((None, NA_HPG, NA_KH, GRID_W, NA_KH * GRID_W),
                               lambda g, bb: (g, 0, 0, 0, 0))],
        out_specs=grp,
        out_shape=jax.ShapeDtypeStruct((b, s, BRANCH), BF16),
        compiler_params=_cparams(("parallel", "parallel")),
        name="na_attn",
    )(q, kt2, v, z, table)


def _mla_in_kernel(x_ref, g_ref, w1_ref, gq_ref, wuq_ref, gkv_ref, wuk_ref, wuvt_ref,
                   rq_ref, rk_ref, q_ref, k_ref, vt_ref, z_ref):
    h = _rmsnorm(x_ref[...], g_ref[...]).astype(BF16)
    c0, c1, c2, c3 = MLA_Q_LORA, MLA_Q_LORA + MLA_KV_LORA, MLA_Q_LORA + MLA_KV_LORA + LANE, \
        MLA_Q_LORA + MLA_KV_LORA + 2 * LANE
    cq = _dot(h, w1_ref[:, :c0])
    ckv = _dot(h, w1_ref[:, c0:c1])
    kp = _dot(h, w1_ref[:, c1:c2])
    kr = _dot(h, w1_ref[:, c2:c3])
    z_ref[...] = _dot(h, w1_ref[:, c3:]).astype(BF16)
    cqn = _rmsnorm(cq, gq_ref[...]).astype(BF16)
    ckvn = _rmsnorm(ckv, gkv_ref[...]).astype(BF16)
    rq = rq_ref[...]
    for hh in range(MLA_HEADS):
        sl = slice(hh * MLA_QK, (hh + 1) * MLA_QK)
        q_ref[:, sl] = (_dot(cqn, wuq_ref[:, sl]) * rq).astype(BF16)
    krr = (kp * rk_ref[:, :LANE] + kr * rk_ref[:, LANE:]).astype(BF16)
    kn = _dot(ckvn, wuk_ref[...]).astype(BF16)
    for hh in range(MLA_HEADS):
        k_ref[:, hh * MLA_QK:hh * MLA_QK + MLA_NOPE] = kn[:, hh * MLA_NOPE:(hh + 1) * MLA_NOPE]
        k_ref[:, hh * MLA_QK + MLA_NOPE:(hh + 1) * MLA_QK] = krr
    vt_ref[...] = _dot_nt(wuvt_ref[...], ckvn).astype(BF16)


def _mla_in(x, g, w1, gq, wuq, gkv, wuk, wuvt, rq, rk, tm):
    b, s, d = x.shape
    n1 = w1.shape[1]
    qk = MLA_HEADS * MLA_QK
    rope_spec = pl.BlockSpec((tm, 2 * LANE), lambda bb, i: (i, 0))
    return pl.pallas_call(
        _mla_in_kernel,
        grid=(b, s // tm),
        in_specs=[_tok_spec(tm, d), _full_spec((1, d)), _full_spec((d, n1)),
                  _full_spec((1, MLA_Q_LORA)), _full_spec((MLA_Q_LORA, qk)),
                  _full_spec((1, MLA_KV_LORA)), _full_spec((MLA_KV_LORA, BRANCH)),
                  _full_spec((BRANCH, MLA_KV_LORA)), rope_spec, rope_spec],
        out_specs=[_tok_spec(tm, qk), _tok_spec(tm, qk),
                   pl.BlockSpec((None, BRANCH, tm), lambda bb, i: (bb, 0, i)), _tok_spec(tm, BRANCH)],
        out_shape=[jax.ShapeDtypeStruct((b, s, qk), BF16), jax.ShapeDtypeStruct((b, s, qk), BF16),
                   jax.ShapeDtypeStruct((b, BRANCH, s), BF16), jax.ShapeDtypeStruct((b, s, BRANCH), BF16)],
        compiler_params=_cparams(("parallel", "parallel")),
        name="mla_in",
    )(x, g, w1, gq, wuq, gkv, wuk, wuvt, rq, rk)


def _mla_attn_kernel(q_ref, k_ref, vt_ref, z_ref, o_ref, *, tk, nk):
    q = q_ref[...]
    tq = q.shape[0]
    m = jnp.full((1, tq), -jnp.inf, F32)
    l = jnp.zeros((1, tq), F32)
    acc = jnp.zeros((MLA_V, tq), F32)
    logits = lambda j: _dot_nt(k_ref[j * tk:(j + 1) * tk, :], q)
    st_next = logits(0)
    for j in range(nk):
        st = st_next
        if j + 1 < nk:
            st_next = logits(j + 1)
        m_new = jnp.maximum(m, jnp.max(st, axis=0, keepdims=True))
        alpha = jnp.exp2(m - m_new)
        p = jnp.exp2(st - m_new)
        l = alpha * l + jnp.sum(p, axis=0, keepdims=True)
        acc = alpha * acc + _dot(vt_ref[:, j * tk:(j + 1) * tk], p.astype(BF16))
        m = m_new
    o = (acc * (1.0 / l)).T
    o_ref[...] = (o * _silu(z_ref[...].astype(F32))).astype(BF16)


def _mla_attn(q, k, vt, z, tq, tk):
    b, _, s = vt.shape
    return pl.pallas_call(
        functools.partial(_mla_attn_kernel, tk=tk, nk=s // tk),
        grid=(b, MLA_HEADS, s // tq),
        in_specs=[pl.BlockSpec((None, tq, MLA_QK), lambda bb, h, i: (bb, i, h)),
                  pl.BlockSpec((None, s, MLA_QK), lambda bb, h, i: (bb, 0, h)),
                  pl.BlockSpec((None, MLA_V, s), lambda bb, h, i: (bb, h, 0)),
                  pl.BlockSpec((None, tq, MLA_V), lambda bb, h, i: (bb, i, h))],
        out_specs=pl.BlockSpec((None, tq, MLA_V), lambda bb, h, i: (bb, i, h)),
        out_shape=jax.ShapeDtypeStruct((b, s, BRANCH), BF16),
        compiler_params=_cparams(("parallel", "parallel", "arbitrary")),
        name="mla_attn",
    )(q, k, vt, z)


def _hg_in_kernel(x_ref, g_ref, w_ref, lbr_ref, q_ref, lf_ref, lb_ref, i_ref, z_ref, *, li):
    h = _rmsnorm(x_ref[...], g_ref[...]).astype(BF16)
    raw = lbr_ref[...]
    e = jnp.exp(raw - jnp.max(raw, axis=0, keepdims=True))
    sm = e / jnp.sum(e, axis=0, keepdims=True)
    lower = jnp.zeros_like(sm[0])
    for dd in range(1, li + 1):
        lower = lower + sm[dd]
    q_ref[...] = _silu(_dot(h, w_ref[:, :BRANCH])).astype(BF16)
    for idx, ref in ((0, lf_ref), (1, lb_ref)):
        lb = lower[idx:idx + 1]
        raw_f = _dot(h, w_ref[:, (1 + idx) * BRANCH:(2 + idx) * BRANCH])
        ref[...] = jnp.log(lb + (1.0 - lb) * _sigmoid(raw_f)) * LOG2E
    i_ref[...] = _dot(h, w_ref[:, 3 * BRANCH:4 * BRANCH]).astype(BF16)
    z_ref[...] = _dot(h, w_ref[:, 4 * BRANCH:]).astype(BF16)


def _hg_in(x, g, w, lb_raw, li, tm):
    b, s, d = x.shape
    bf = jax.ShapeDtypeStruct((b, s, BRANCH), BF16)
    f32 = jax.ShapeDtypeStruct((b, s, BRANCH), F32)
    return pl.pallas_call(
        functools.partial(_hg_in_kernel, li=li),
        grid=(b, s // tm),
        in_specs=[_tok_spec(tm, d), _full_spec((1, d)), _full_spec((d, 5 * BRANCH)),
                  _full_spec((DEPTH, 2, BRANCH))],
        out_specs=[_tok_spec(tm, BRANCH)] * 5,
        out_shape=[bf, f32, f32, bf, bf],
        compiler_params=_cparams(("parallel", "parallel")),
        name="hg_in",
    )(x, g, w, lb_raw)


def _hg_level_operands(q, k, lf2, cum, n, fwd, sel):
    L = HG_CHUNK
    half = n // 2
    ref_row = half - 1 if fwd else half
    if n >= 16:
        zero = jnp.zeros((half, HG_DK), BF16)
        qt_parts, kt_parts = [], []
        for blk in range(L // n):
            cm = cum[blk * n + ref_row:blk * n + ref_row + 1]
            for is_upper in (False, True):
                r0 = blk * n + (half if is_upper else 0)
                seg = slice(r0, r0 + half)
                e = jnp.exp2(-jnp.abs(cum[seg] - cm))
                if is_upper == fwd:
                    qt_parts.append((q[seg] * e).astype(BF16))
                    kt_parts.append(zero)
                else:
                    qt_parts.append(zero)
                    kt_parts.append((k[seg] * e).astype(BF16))
        return jnp.concatenate(qt_parts, axis=0), jnp.concatenate(kt_parts, axis=0)
    q_side = sel["upper"][n] if fwd else jnp.logical_not(sel["upper"][n])
    if n == 2:
        gl = jnp.where(q_side, lf2, 0.0)
    else:
        c3 = cum.reshape(L // n, n, HG_DK)
        cm = jnp.broadcast_to(c3[:, ref_row:ref_row + 1, :], c3.shape).reshape(L, HG_DK)
        gl = -jnp.abs(cum - cm)
    e = jnp.exp2(gl)
    return jnp.where(q_side, q * e, 0.0).astype(BF16), jnp.where(q_side, 0.0, k * e).astype(BF16)


def _hg_chunks(items, sel):
    L = HG_CHUNK
    cats = []
    for q, lf2, v, tri3, fwd in items:
        hi = lf2.astype(BF16)
        r1 = lf2 - hi.astype(F32)
        mid = r1.astype(BF16)
        lo = (r1 - mid.astype(F32)).astype(BF16)
        cats.append(jnp.concatenate([hi, mid, lo], axis=0))
    cums = [_dot(it[3], cat) for it, cat in zip(items, cats)]
    ks = [1.0 - jnp.exp2(it[1]) for it in items]
    accs = [jnp.where(sel["eye"], jnp.sum(it[0] * k, axis=-1, keepdims=True), 0.0)
            for it, k in zip(items, ks)]
    for n in HG_LEVELS:
        ops = [_hg_level_operands(it[0], k, it[1], cum, n, it[4], sel)
               for it, k, cum in zip(items, ks, cums)]
        prods = [_dot_nt(qt, kt) for qt, kt in ops]
        accs = [jnp.where(sel["own"][n], p, a) for p, a in zip(prods, accs)]
    o_intras = [_dot(a.astype(BF16), it[2]) for a, it in zip(accs, items)]
    edges = [cum[L - 1:L] if it[4] else cum[0:1] for it, cum in zip(items, cums)]
    khs = [(k * jnp.exp2(e - cum)).astype(BF16) for k, e, cum in zip(ks, edges, cums)]
    us = [_dot_tn(it[2], kh) for it, kh in zip(items, khs)]
    qbs = [(it[0] * jnp.exp2(cum)).astype(BF16) for it, cum in zip(items, cums)]
    return [(o, qb, u, jnp.exp2(e)) for o, qb, u, e in zip(o_intras, qbs, us, edges)]


def _hg_scan_kernel(q_ref, lf_ref, lb_ref, v_ref, z_ref, g_ref, o_ref,
                    of_ref, ob_ref, sf_ref, sb_ref, *, nchunks, fin_rows):
    L = HG_CHUNK
    t_i = lax.broadcasted_iota(jnp.int32, (L, L), 0)
    s_i = lax.broadcasted_iota(jnp.int32, (L, L), 1)
    x_i = t_i ^ s_i
    row = lax.broadcasted_iota(jnp.int32, (L, HG_DK), 0)
    sel = {"eye": t_i == s_i,
           "own": {n: (x_i < n) & (x_i >= n // 2) for n in HG_LEVELS},
           "upper": {n: (row & (n - 1)) >= n // 2 for n in HG_LEVELS if n < 16}}
    t3 = lax.broadcasted_iota(jnp.int32, (L, 3 * L), 0)
    u3 = lax.broadcasted_iota(jnp.int32, (L, 3 * L), 1) & (L - 1)
    tri_f = (u3 <= t3).astype(BF16)
    tri_b = (u3 >= t3).astype(BF16)
    sf_ref[...] = jnp.zeros_like(sf_ref)
    sb_ref[...] = jnp.zeros_like(sb_ref)

    def body(i, carry):
        dirs = ((True, lf_ref, sf_ref, of_ref, tri_f), (False, lb_ref, sb_ref, ob_ref, tri_b))
        items, rows_of = [], []
        for fwd, gate_ref, _, _, tri in dirs:
            for jj in range(HG_UNROLL):
                c = i * HG_UNROLL + jj
                c = c if fwd else nchunks - 1 - c
                rows = pl.ds(pl.multiple_of(c * L, L), L)
                rows_of.append(rows)
                items.append((q_ref[rows, :].astype(F32), gate_ref[rows, :], v_ref[rows, :], tri, fwd))
        parts = _hg_chunks(items, sel)
        sts = [d[2][...] for d in dirs]
        for jj in range(HG_UNROLL):
            for di, (_, _, _, out_ref, _) in enumerate(dirs):
                o_intra, qb, u, dec = parts[di * HG_UNROLL + jj]
                out_ref[rows_of[di * HG_UNROLL + jj], :] = o_intra + _dot_nt(qb, sts[di].astype(BF16))
                sts[di] = sts[di] * dec + u
        for d, st in zip(dirs, sts):
            d[2][...] = st
        return carry

    lax.fori_loop(0, nchunks // HG_UNROLL, body, 0)

    def fin(i, carry):
        rows = pl.ds(pl.multiple_of(i * fin_rows, fin_rows), fin_rows)
        o = of_ref[rows, :] + ob_ref[rows, :]
        y = _rmsnorm(o, g_ref[...])
        o_ref[rows, :] = (y * _silu(z_ref[rows, :].astype(F32))).astype(BF16)
        return carry

    lax.fori_loop(0, (nchunks * L) // fin_rows, fin, 0)


def _hg_scan(q, lf, lb, v, z, g_out):
    b, s, _ = q.shape
    head = pl.BlockSpec((None, s, HG_DK), lambda bb, h: (bb, 0, h))
    fin_rows = min(256, s)
    return pl.pallas_call(
        functools.partial(_hg_scan_kernel, nchunks=s // HG_CHUNK, fin_rows=fin_rows),
        grid=(b, HG_HEADS),
        in_specs=[head] * 5 + [pl.BlockSpec((1, HG_DK), lambda bb, h: (0, h))],
        out_specs=head,
        out_shape=jax.ShapeDtypeStruct((b, s, BRANCH), BF16),
        scratch_shapes=[pltpu.VMEM((s, HG_DK), F32), pltpu.VMEM((s, HG_DK), F32),
                        pltpu.VMEM((HG_DK, HG_DK), F32), pltpu.VMEM((HG_DK, HG_DK), F32)],
        compiler_params=_cparams(("parallel", "parallel")),
        name="hg_scan",
    )(q, lf, lb, v, z, g_out)


def _rot_half_cols(w):
    half = MLA_ROPE // 2
    return jnp.concatenate([-w[..., half:], w[..., :half]], axis=-1)


def _prep(s, norm_g, fn_w_in, fn_w_mix, fn_w_out, na_w_in, na_rpb, na_w_out,
          mla_w_in, mla_g_q, mla_w_uq, mla_g_kv, mla_w_ukv, mla_w_out,
          hg_w_in, hg_lb_raw, hg_g_out, hg_w_out, ple_w, ple_gate_w, final_g, tk):
    bf = lambda t: t.astype(BF16)
    row = lambda t: t.reshape(1, -1).astype(F32)
    w = {}
    w["norm_g"] = [row(norm_g[i]) for i in range(DEPTH)]
    w["final_g"] = row(final_g)
    w["ple_w"] = [bf(ple_w[i]) for i in range(DEPTH)]
    w["gate_w"] = [bf(ple_gate_w[i]) for i in range(DEPTH)]
    w["w_out"] = [bf(fn_w_out[0]), bf(na_w_out[0]), bf(mla_w_out[0]), bf(hg_w_out[0])]
    w["fn_w_in"] = bf(fn_w_in[0])
    w["fn_w2"] = _fn_fold(fn_w_mix[0].astype(F32), s)
    w["dft"] = _dft_mats(s, tk)
    w["na_w_in"] = bf(na_w_in[0])
    w["na_wkt"] = bf(na_w_in[0][:, BRANCH:2 * BRANCH].T)
    w["na_table"] = _na_bias_table(na_rpb[0].astype(F32))
    wi = mla_w_in[0]
    c0, c1, c2 = MLA_Q_LORA, MLA_Q_LORA + MLA_KV_LORA, MLA_Q_LORA + MLA_KV_LORA + MLA_ROPE
    wkpe = wi[:, c1:c2]
    wrot = _rot_half_cols(wkpe)
    w["mla_w1"] = bf(jnp.concatenate([wi[:, :c1], wkpe, wkpe, wrot, wrot, wi[:, c2:]], axis=1))
    wq = mla_w_uq[0].reshape(MLA_Q_LORA, MLA_HEADS, MLA_NOPE + MLA_ROPE)
    wq_pe = wq[..., MLA_NOPE:]
    w["mla_wuq"] = bf(jnp.concatenate([wq[..., :MLA_NOPE], wq_pe, _rot_half_cols(wq_pe)], axis=-1)
                      .reshape(MLA_Q_LORA, MLA_HEADS * MLA_QK))
    wkv = mla_w_ukv[0].reshape(MLA_KV_LORA, MLA_HEADS, MLA_NOPE + MLA_V)
    w["mla_wuk"] = bf(wkv[..., :MLA_NOPE].reshape(MLA_KV_LORA, BRANCH))
    w["mla_wuvt"] = bf(wkv[..., MLA_NOPE:].reshape(MLA_KV_LORA, BRANCH).T)
    w["mla_gq"] = row(mla_g_q[0])
    w["mla_gkv"] = row(mla_g_kv[0])
    half = MLA_ROPE // 2
    inv = ROPE_THETA ** (-jnp.arange(half, dtype=F32) / half)
    ang = jnp.arange(s, dtype=F32)[:, None] * inv[None, :]
    cos, sin = jnp.cos(ang), jnp.sin(ang)
    cos64 = jnp.concatenate([cos, cos], axis=-1)
    sin64 = jnp.concatenate([sin, sin], axis=-1)
    scale = (MLA_NOPE + MLA_ROPE) ** -0.5 * math.log2(math.e)
    w["mla_rq"] = scale * jnp.concatenate([jnp.ones((s, MLA_NOPE), F32), cos64, sin64], axis=-1)
    w["mla_rk"] = jnp.concatenate([cos64, cos64, sin64, sin64], axis=-1)
    w["hg_w_in"] = bf(hg_w_in[0])
    w["hg_lb_raw"] = hg_lb_raw.astype(F32)
    w["hg_g_out"] = row(hg_g_out[0])
    return w


def _trunk(x, p, w, tm, tk, tq, tkv):
    def finish(x, o, li, final=False):
        return _out_proj(x, o, p, li, w["w_out"][li], w["gate_w"][li], w["ple_w"][li],
                         w["final_g"] if final else None, tm)

    u, z = _fn_in(x, w["norm_g"][0], w["fn_w_in"], tm)
    o = _fn_dft(w["dft"], u, z, w["fn_w2"], tk)
    x = finish(x, o, 0)
    q, kt, v, z = _na_in(x, w["norm_g"][1], w["na_w_in"], w["na_wkt"], tm)
    kt_shift = jnp.concatenate([kt[:, :, GRID_W:], jnp.zeros_like(kt[:, :, :GRID_W])], axis=2)
    o = _na_attn(q, jnp.stack([kt, kt_shift], axis=1), v, z, w["na_table"])
    x = finish(x, o, 1)
    q, k, vt, z = _mla_in(x, w["norm_g"][2], w["mla_w1"], w["mla_gq"], w["mla_wuq"], w["mla_gkv"],
                          w["mla_wuk"], w["mla_wuvt"], w["mla_rq"], w["mla_rk"], tm)
    o = _mla_attn(q, k, vt, z, tq, tkv)
    x = finish(x, o, 2)
    q, lf, lb, v, z = _hg_in(x, w["norm_g"][3], w["hg_w_in"], w["hg_lb_raw"], 3, tm)
    o = _hg_scan(q, lf, lb, v, z, w["hg_g_out"])
    return finish(x, o, 3, final=True)


def kernel(x_prompt, x_sample, p_prompt, p_sample, norm_g, fn_w_in, fn_w_mix, fn_w_out, na_w_in, na_rpb, na_w_out, mla_w_in, mla_g_q, mla_w_uq, mla_g_kv, mla_w_ukv, mla_w_out, hg_w_in, hg_lb_raw, hg_g_out, hg_w_out, ple_w, ple_gate_w, final_g):
    s = x_prompt.shape[1]
    assert x_sample.shape[1] == s
    tm = min(512, s)
    tk = min(512, s)
    tq = min(512, s)
    tkv = min(1024, s)
    w = _prep(s, norm_g, fn_w_in, fn_w_mix, fn_w_out, na_w_in, na_rpb, na_w_out,
              mla_w_in, mla_g_q, mla_w_uq, mla_g_kv, mla_w_ukv, mla_w_out,
              hg_w_in, hg_lb_raw, hg_g_out, hg_w_out, ple_w, ple_gate_w, final_g, tk)
    y_prompt = _trunk(x_prompt, p_prompt, w, tm, tk, tq, tkv)
    y_sample = _trunk(x_sample, p_sample, w, tm, tk, tq, tkv)
    return (y_prompt, y_sample)
```

```python
import functools
import math

import numpy as np
import jax
import jax.numpy as jnp
from jax import lax
from jax.experimental import pallas as pl
from jax.experimental.pallas import tpu as pltpu

F32 = jnp.float32
BF16 = jnp.bfloat16

D_MODEL = 1024
DEPTH = 4
BRANCH = 1024
PLE_DIM = 256
GRID_W = 64
EPS = 1e-6
LANE = 128
VMEM_LIMIT = 56 * 1024 * 1024

FN_GROUPS = 8
FN_GDIM = BRANCH // FN_GROUPS

NA_HEADS = 32
NA_HDIM = BRANCH // NA_HEADS
NA_KH = 8
NA_KW = 16
NA_HPG = LANE // NA_HDIM
NA_GROUPS = NA_HEADS // NA_HPG
NA_NEG = -1e30

MLA_HEADS = 8
MLA_NOPE = 128
MLA_ROPE = 64
MLA_V = BRANCH // MLA_HEADS
MLA_Q_LORA = 384
MLA_KV_LORA = 256
MLA_QK = MLA_NOPE + 2 * MLA_ROPE
ROPE_THETA = 10000.0

HG_HEADS = 8
HG_DK = BRANCH // HG_HEADS
HG_CHUNK = 64
HG_LEVELS = (64, 32, 16, 8, 4, 2)
HG_UNROLL = 4
NA_UNROLL = 4
LOG2E = math.log2(math.e)


def _cparams(sem):
    return pltpu.CompilerParams(dimension_semantics=sem, vmem_limit_bytes=VMEM_LIMIT)


def _rmsnorm(x, g):
    ms = jnp.mean(x * x, axis=-1, keepdims=True)
    return x * lax.rsqrt(ms + EPS) * g


def _sigmoid(x):
    return 1.0 / (1.0 + jnp.exp(-x))


def _silu(x):
    return x * _sigmoid(x)


def _dot(a, b):
    return jnp.dot(a, b, preferred_element_type=F32)


def _dot_nt(a, b):
    return lax.dot_general(a, b, (((1,), (1,)), ((), ())), preferred_element_type=F32)


def _dot_tn(a, b):
    return lax.dot_general(a, b, (((0,), (0,)), ((), ())), preferred_element_type=F32)


def _tok_spec(tm, width):
    return pl.BlockSpec((None, tm, width), lambda b, i: (b, i, 0))


def _full_spec(shape):
    nd = len(shape)
    return pl.BlockSpec(shape, lambda *_: (0,) * nd)


def _fn_in_kernel(x_ref, g_ref, w_ref, u_ref, z_ref):
    h = _rmsnorm(x_ref[...], g_ref[...]).astype(BF16)
    u_ref[...] = _dot(h, w_ref[:, :BRANCH]).astype(BF16)
    z_ref[...] = _dot(h, w_ref[:, BRANCH:]).astype(BF16)


def _fn_in(x, g, w, tm):
    b, s, d = x.shape
    return pl.pallas_call(
        _fn_in_kernel,
        grid=(b, s // tm),
        in_specs=[_tok_spec(tm, d), _full_spec((1, d)), _full_spec((d, 2 * BRANCH))],
        out_specs=[_tok_spec(tm, BRANCH)] * 2,
        out_shape=[jax.ShapeDtypeStruct((b, s, BRANCH), BF16)] * 2,
        compiler_params=_cparams(("parallel", "parallel")),
        name="fn_in",
    )(x, g, w)


def _fn_fold_kernel(c_ref, s_ref, w_ref, o_ref, *, scale):
    w = w_ref[...]
    hi = lax.Precision.HIGHEST
    wc = (scale * jnp.dot(c_ref[...], w, precision=hi, preferred_element_type=F32)).astype(BF16)
    ws = (scale * jnp.dot(s_ref[...], w, precision=hi, preferred_element_type=F32)).astype(BF16)
    o_ref[:FN_GDIM, :FN_GDIM] = wc
    o_ref[:FN_GDIM, FN_GDIM:] = wc
    o_ref[FN_GDIM:, :FN_GDIM] = -ws
    o_ref[FN_GDIM:, FN_GDIM:] = ws


def _fn_fold(w_mix, s):
    c = jnp.arange(FN_GDIM, dtype=jnp.int32)
    ang = ((c[:, None] * c[None, :]) % FN_GDIM).astype(F32) * (2.0 * math.pi / FN_GDIM)
    scale = 1.0 / math.sqrt(s * FN_GDIM)
    gspec = pl.BlockSpec((None, FN_GDIM, FN_GDIM), lambda g: (g, 0, 0))
    return pl.pallas_call(
        functools.partial(_fn_fold_kernel, scale=scale),
        grid=(FN_GROUPS,),
        in_specs=[_full_spec((FN_GDIM, FN_GDIM))] * 2 + [gspec],
        out_specs=pl.BlockSpec((None, 2 * FN_GDIM, 2 * FN_GDIM), lambda g: (g, 0, 0)),
        out_shape=jax.ShapeDtypeStruct((FN_GROUPS, 2 * FN_GDIM, 2 * FN_GDIM), BF16),
        name="fn_fold",
    )(jnp.cos(ang), jnp.sin(ang), w_mix)


def _dft_mats(s, tk):
    nt = s // 2 // tk
    k = jnp.arange(s // 2, dtype=jnp.int32)
    pos = jnp.arange(s, dtype=jnp.int32)
    ang = ((k[:, None] * pos[None, :]) % s).astype(F32) * (2.0 * math.pi / s)
    c = jnp.cos(ang).astype(BF16).reshape(nt, tk, s)
    sn = jnp.sin(ang).astype(BF16).reshape(nt, tk, s)
    nyq = jnp.where(pos % 2 == 0, 1.0, -1.0).astype(BF16)
    e = jnp.concatenate([nyq[None, :], jnp.zeros((7, s), BF16)], axis=0)
    r = np.arange(tk)
    rev = np.zeros((tk, tk), np.float32)
    rev[r[1:], tk - r[1:]] = 1.0
    return jnp.concatenate([c, sn], axis=1), e, jnp.asarray(rev, BF16)


def _fn_dft_kernel(m_ref, e_ref, rev_ref, u_ref, zd_ref, zm_ref, w2_ref, o_ref, r_ref, carry_ref, *, tk):
    u = u_ref[...]

    @pl.when(pl.program_id(1) == 0)
    def _():
        ev = _dot(e_ref[...], u).astype(BF16)
        for g in range(FN_GROUPS):
            sl = slice(g * FN_GDIM, (g + 1) * FN_GDIM)
            ab = jnp.concatenate([ev[:, sl], jnp.zeros_like(ev[:, sl])], axis=1)
            carry_ref[:, sl] = _dot(ab, w2_ref[g])[:, FN_GDIM:]

    r = _dot(m_ref[...], u)
    a = r[:tk].astype(BF16)
    bm = r[tk:].astype(BF16)
    carry = carry_ref[0:1, :]
    for g in range(FN_GROUPS):
        sl = slice(g * FN_GDIM, (g + 1) * FN_GDIM)
        ab = jnp.concatenate([a[:, sl], bm[:, sl]], axis=1)
        y2 = _dot(ab, w2_ref[g])
        o_ref[0, :, sl] = (y2[:, :FN_GDIM] * _silu(zd_ref[:, sl].astype(F32))).astype(BF16)
        r_ref[:, sl] = y2[:, FN_GDIM:].astype(BF16)
        carry_ref[:, sl] = y2[0:8, FN_GDIM:]
    mirror = _dot(rev_ref[...], r_ref[...])
    row0 = lax.broadcasted_iota(jnp.int32, (tk, BRANCH), 0) == 0
    mirror = jnp.where(row0, carry, mirror)
    o_ref[1] = (mirror * _silu(zm_ref[...].astype(F32))).astype(BF16)


def _fn_dft(mats, u, z, w2, tk):
    m, e, rev = mats
    b, s, _ = u.shape
    nt = s // 2 // tk
    return pl.pallas_call(
        functools.partial(_fn_dft_kernel, tk=tk),
        grid=(b, nt),
        in_specs=[pl.BlockSpec((None, 2 * tk, s), lambda bb, jj: (nt - 1 - jj, 0, 0)),
                  _full_spec((8, s)), _full_spec((tk, tk)),
                  pl.BlockSpec((None, s, BRANCH), lambda bb, jj: (bb, 0, 0)),
                  pl.BlockSpec((None, tk, BRANCH), lambda bb, jj: (bb, nt - 1 - jj, 0)),
                  pl.BlockSpec((None, tk, BRANCH), lambda bb, jj: (bb, nt + jj, 0)),
                  _full_spec((FN_GROUPS, 2 * FN_GDIM, 2 * FN_GDIM))],
        out_specs=pl.BlockSpec((None, 2, tk, BRANCH), lambda bb, jj: (bb, 0, nt - 1 - jj, 0)),
        out_shape=jax.ShapeDtypeStruct((b, 2, s // 2, BRANCH), BF16),
        scratch_shapes=[pltpu.VMEM((tk, BRANCH), BF16), pltpu.VMEM((8, BRANCH), F32)],
        compiler_params=_cparams(("parallel", "arbitrary")),
        name="fn_dft",
    )(m, e, rev, u, z, z, w2)


def _out_kernel(x_ref, o_ref, p_ref, wo_ref, gw_ref, pw_ref, *rest, final):
    y_ref = rest[-1]
    x1 = x_ref[...] + _dot(o_ref[...], wo_ref[...])
    gate = _sigmoid(_dot(x1.astype(BF16), gw_ref[...]))
    x2 = x1 + gate * _dot(p_ref[...].astype(BF16), pw_ref[...])
    if final:
        x2 = _rmsnorm(x2, rest[0][...])
    y_ref[...] = x2


def _out_proj(x, o, p, li, wo, gw, pw, final_g, tm):
    b, s, d = x.shape
    final = final_g is not None
    if o.ndim == 4:
        nt = s // 2 // tm
        assert o.shape == (b, 2, nt * tm, BRANCH)
        o_spec = pl.BlockSpec((None, None, tm, BRANCH),
                              lambda bb, i: (bb, i // nt, jnp.where(i < nt, i, 2 * nt - 1 - i), 0))
    else:
        o_spec = _tok_spec(tm, BRANCH)
    in_specs = [_tok_spec(tm, d), o_spec,
                pl.BlockSpec((None, None, tm, PLE_DIM), lambda bb, i: (li, bb, i, 0)),
                _full_spec((BRANCH, d)), _full_spec((d, d)), _full_spec((PLE_DIM, d))]
    args = [x, o, p, wo, gw, pw]
    if final:
        in_specs.append(_full_spec((1, d)))
        args.append(final_g)
    return pl.pallas_call(
        functools.partial(_out_kernel, final=final),
        grid=(b, s // tm),
        in_specs=in_specs,
        out_specs=_tok_spec(tm, d),
        out_shape=jax.ShapeDtypeStruct((b, s, d), F32),
        compiler_params=_cparams(("parallel", "parallel")),
        name="out_proj",
    )(*args)


def _na_in_kernel(x_ref, g_ref, w_ref, wkt_ref, q_ref, kt_ref, v_ref, z_ref):
    h = _rmsnorm(x_ref[...], g_ref[...]).astype(BF16)
    q_ref[...] = (_dot(h, w_ref[:, :BRANCH]) * (NA_HDIM ** -0.5 * LOG2E)).astype(BF16)
    kt_ref[...] = _dot_nt(wkt_ref[...], h).astype(BF16)
    v_ref[...] = _dot(h, w_ref[:, 2 * BRANCH:3 * BRANCH]).astype(BF16)
    z_ref[...] = _dot(h, w_ref[:, 3 * BRANCH:]).astype(BF16)


def _na_in(x, g, w, wkt, tm):
    b, s, d = x.shape
    tok = jax.ShapeDtypeStruct((b, s, BRANCH), BF16)
    return pl.pallas_call(
        _na_in_kernel,
        grid=(b, s // tm),
        in_specs=[_tok_spec(tm, d), _full_spec((1, d)), _full_spec((d, 4 * BRANCH)),
                  _full_spec((BRANCH, d))],
        out_specs=[_tok_spec(tm, BRANCH),
                   pl.BlockSpec((None, BRANCH, tm), lambda bb, i: (bb, 0, i)),
                   _tok_spec(tm, BRANCH), _tok_spec(tm, BRANCH)],
        out_shape=[tok, jax.ShapeDtypeStruct((b, BRANCH, s), BF16), tok, tok],
        compiler_params=_cparams(("parallel", "parallel")),
        name="na_in",
    )(x, g, w, wkt)


def _na_bias_table(rpb):
    d = np.arange(NA_KH)
    i = np.arange(NA_KH)
    ro = i[None, :] - d[:, None] + (NA_KH - 1)
    c = np.arange(GRID_W)
    kc = np.arange(GRID_W)
    win = np.clip(c - NA_KW // 2, 0, GRID_W - NA_KW)
    valid = (kc[None, :] >= win[:, None]) & (kc[None, :] < win[:, None] + NA_KW)
    pad = GRID_W - NA_KW
    r1 = jnp.pad(rpb[:, ro, :] * LOG2E, ((0, 0), (0, 0), (0, 0), (pad, pad)))
    cols = [r1[..., GRID_W - 1 - cc:2 * GRID_W - 1 - cc] for cc in range(GRID_W)]
    bias = jnp.stack(cols, axis=2)
    bias = jnp.where(jnp.asarray(valid)[None, None, :, None, :], bias, NA_NEG)
    return bias.reshape(NA_GROUPS, NA_HPG, NA_KH, GRID_W, NA_KH * GRID_W).astype(F32)


def _na_attn_kernel(q_ref, kt_ref, v_ref, z_ref, t_ref, o_ref, *, rows):
    lane = lax.broadcasted_iota(jnp.int32, (GRID_W, LANE), 1)
    head_masks = [(lane >= hh * NA_HDIM) & (lane < (hh + 1) * NA_HDIM) for hh in range(NA_HPG)]
    nkeys = NA_KH * GRID_W

    def logits(r):
        rs = jnp.clip(r - NA_KH // 2, 0, rows - NA_KH)
        d = r - rs
        par = rs & 1
        q_r = q_ref[pl.ds(pl.multiple_of(r * GRID_W, GRID_W), GRID_W), :]
        zero = jnp.zeros_like(q_r)
        qs = jnp.concatenate([jnp.where(m, q_r, zero) for m in head_masks], axis=0)
        k0 = pl.multiple_of((rs - par) * GRID_W, LANE)
        kwin = kt_ref[par, :, pl.ds(k0, nkeys)]
        bias = jnp.concatenate([t_ref[hh, d] for hh in range(NA_HPG)], axis=0)
        return _dot(qs, kwin) + bias

    def probs(sc):
        p = jnp.exp2(sc - jnp.max(sc, axis=-1, keepdims=True))
        return p.astype(BF16), jnp.sum(p, axis=-1, keepdims=True)

    def values(r, p, l):
        rs = jnp.clip(r - NA_KH // 2, 0, rows - NA_KH)
        vwin = v_ref[pl.ds(pl.multiple_of(rs * GRID_W, GRID_W), nkeys), :]
        o = _dot(p, vwin) * (1.0 / l)
        acc = jnp.zeros((GRID_W, LANE), F32)
        for hh in range(NA_HPG):
            acc = jnp.where(head_masks[hh], o[hh * GRID_W:(hh + 1) * GRID_W], acc)
        q0 = pl.multiple_of(r * GRID_W, GRID_W)
        zz = z_ref[pl.ds(q0, GRID_W), :].astype(F32)
        o_ref[pl.ds(q0, GRID_W), :] = (acc * _silu(zz)).astype(BF16)

    def body(i, carry):
        rws = [i * NA_UNROLL + u for u in range(NA_UNROLL)]
        scs = [logits(r) for r in rws]
        pls = [probs(sc) for sc in scs]
        for r, (p, l) in zip(rws, pls):
            values(r, p, l)
        return carry

    lax.fori_loop(0, rows // NA_UNROLL, body, 0)


def _na_attn(q, kt2, v, z, table):
    b, s, _ = q.shape
    rows = s // GRID_W
    assert rows >= NA_KH
    grp = pl.BlockSpec((None, s, LANE), lambda g, bb: (bb, 0, g))
    return pl.pallas_call(
        functools.partial(_na_attn_kernel, rows=rows),
        grid=(NA_GROUPS, b),
        in_specs=[grp,
                  pl.BlockSpec((None, 2, LANE, s), lambda g, bb: (bb, 0, g, 0)),
                  grp, grp,
                  pl.BlockSpec((None, NA_HPG, NA_KH, GRID_W, NA_KH * GRID_W),
                               lambda g, bb: (g, 0, 0, 0, 0))],
        out_specs=grp,
        out_shape=jax.ShapeDtypeStruct((b, s, BRANCH), BF16),
        compiler_params=_cparams(("parallel", "parallel")),
        name="na_attn",
    )(q, kt2, v, z, table)


def _mla_in_kernel(x_ref, g_ref, w1_ref, gq_ref, wuq_ref, gkv_ref, wuk_ref, wuvt_ref,
                   rq_ref, rk_ref, q_ref, k_ref, vt_ref, z_ref):
    h = _rmsnorm(x_ref[...], g_ref[...]).astype(BF16)
    c0, c1, c2, c3 = MLA_Q_LORA, MLA_Q_LORA + MLA_KV_LORA, MLA_Q_LORA + MLA_KV_LORA + LANE, \
        MLA_Q_LORA + MLA_KV_LORA + 2 * LANE
    cq = _dot(h, w1_ref[:, :c0])
    ckv = _dot(h, w1_ref[:, c0:c1])
    kp = _dot(h, w1_ref[:, c1:c2])
    kr = _dot(h, w1_ref[:, c2:c3])
    z_ref[...] = _dot(h, w1_ref[:, c3:]).astype(BF16)
    cqn = _rmsnorm(cq, gq_ref[...]).astype(BF16)
    ckvn = _rmsnorm(ckv, gkv_ref[...]).astype(BF16)
    rq = rq_ref[...]
    for hh in range(MLA_HEADS):
        sl = slice(hh * MLA_QK, (hh + 1) * MLA_QK)
        q_ref[:, sl] = (_dot(cqn, wuq_ref[:, sl]) * rq).astype(BF16)
    krr = (kp * rk_ref[:, :LANE] + kr * rk_ref[:, LANE:]).astype(BF16)
    kn = _dot(ckvn, wuk_ref[...]).astype(BF16)
    for hh in range(MLA_HEADS):
        k_ref[:, hh * MLA_QK:hh * MLA_QK + MLA_NOPE] = kn[:, hh * MLA_NOPE:(hh + 1) * MLA_NOPE]
        k_ref[:, hh * MLA_QK + MLA_NOPE:(hh + 1) * MLA_QK] = krr
    vt_ref[...] = _dot_nt(wuvt_ref[...], ckvn).astype(BF16)


def _mla_in(x, g, w1, gq, wuq, gkv, wuk, wuvt, rq, rk, tm):
    b, s, d = x.shape
    n1 = w1.shape[1]
    qk = MLA_HEADS * MLA_QK
    rope_spec = pl.BlockSpec((tm, 2 * LANE), lambda bb, i: (i, 0))
    return pl.pallas_call(
        _mla_in_kernel,
        grid=(b, s // tm),
        in_specs=[_tok_spec(tm, d), _full_spec((1, d)), _full_spec((d, n1)),
                  _full_spec((1, MLA_Q_LORA)), _full_spec((MLA_Q_LORA, qk)),
                  _full_spec((1, MLA_KV_LORA)), _full_spec((MLA_KV_LORA, BRANCH)),
                  _full_spec((BRANCH, MLA_KV_LORA)), rope_spec, rope_spec],
        out_specs=[_tok_spec(tm, qk), _tok_spec(tm, qk),
                   pl.BlockSpec((None, BRANCH, tm), lambda bb, i: (bb, 0, i)), _tok_spec(tm, BRANCH)],
        out_shape=[jax.ShapeDtypeStruct((b, s, qk), BF16), jax.ShapeDtypeStruct((b, s, qk), BF16),
                   jax.ShapeDtypeStruct((b, BRANCH, s), BF16), jax.ShapeDtypeStruct((b, s, BRANCH), BF16)],
        compiler_params=_cparams(("parallel", "parallel")),
        name="mla_in",
    )(x, g, w1, gq, wuq, gkv, wuk, wuvt, rq, rk)


def _mla_attn_kernel(q_ref, k_ref, vt_ref, z_ref, o_ref, *, tk, nk):
    q = q_ref[...]
    tq = q.shape[0]
    m = jnp.full((1, tq), -jnp.inf, F32)
    l = jnp.zeros((1, tq), F32)
    acc = jnp.zeros((MLA_V, tq), F32)
    logits = lambda j: _dot_nt(k_ref[j * tk:(j + 1) * tk, :], q)
    st_next = logits(0)
    for j in range(nk):
        st = st_next
        if j + 1 < nk:
            st_next = logits(j + 1)
        m_new = jnp.maximum(m, jnp.max(st, axis=0, keepdims=True))
        alpha = jnp.exp2(m - m_new)
        p = jnp.exp2(st - m_new)
        l = alpha * l + jnp.sum(p, axis=0, keepdims=True)
        acc = alpha * acc + _dot(vt_ref[:, j * tk:(j + 1) * tk], p.astype(BF16))
        m = m_new
    o = (acc * (1.0 / l)).T
    o_ref[...] = (o * _silu(z_ref[...].astype(F32))).astype(BF16)


def _mla_attn(q, k, vt, z, tq, tk):
    b, _, s = vt.shape
    return pl.pallas_call(
        functools.partial(_mla_attn_kernel, tk=tk, nk=s // tk),
        grid=(b, MLA_HEADS, s // tq),
        in_specs=[pl.BlockSpec((None, tq, MLA_QK), lambda bb, h, i: (bb, i, h)),
                  pl.BlockSpec((None, s, MLA_QK), lambda bb, h, i: (bb, 0, h)),
                  pl.BlockSpec((None, MLA_V, s), lambda bb, h, i: (bb, h, 0)),
                  pl.BlockSpec((None, tq, MLA_V), lambda bb, h, i: (bb, i, h))],
        out_specs=pl.BlockSpec((None, tq, MLA_V), lambda bb, h, i: (bb, i, h)),
        out_shape=jax.ShapeDtypeStruct((b, s, BRANCH), BF16),
        compiler_params=_cparams(("parallel", "parallel", "arbitrary")),
        name="mla_attn",
    )(q, k, vt, z)


def _hg_in_kernel(x_ref, g_ref, w_ref, lbr_ref, q_ref, lf_ref, lb_ref, i_ref, z_ref, *, li):
    h = _rmsnorm(x_ref[...], g_ref[...]).astype(BF16)
    raw = lbr_ref[...]
    e = jnp.exp(raw - jnp.max(raw, axis=0, keepdims=True))
    sm = e / jnp.sum(e, axis=0, keepdims=True)
    lower = jnp.zeros_like(sm[0])
    for dd in range(1, li + 1):
        lower = lower + sm[dd]
    q_ref[...] = _silu(_dot(h, w_ref[:, :BRANCH])).astype(BF16)
    for idx, ref in ((0, lf_ref), (1, lb_ref)):
        lb = lower[idx:idx + 1]
        raw_f = _dot(h, w_ref[:, (1 + idx) * BRANCH:(2 + idx) * BRANCH])
        ref[...] = jnp.log(lb + (1.0 - lb) * _sigmoid(raw_f)) * LOG2E
    i_ref[...] = _dot(h, w_ref[:, 3 * BRANCH:4 * BRANCH]).astype(BF16)
    z_ref[...] = _dot(h, w_ref[:, 4 * BRANCH:]).astype(BF16)


def _hg_in(x, g, w, lb_raw, li, tm):
    b, s, d = x.shape
    bf = jax.ShapeDtypeStruct((b, s, BRANCH), BF16)
    f32 = jax.ShapeDtypeStruct((b, s, BRANCH), F32)
    return pl.pallas_call(
        functools.partial(_hg_in_kernel, li=li),
        grid=(b, s // tm),
        in_specs=[_tok_spec(tm, d), _full_spec((1, d)), _full_spec((d, 5 * BRANCH)),
                  _full_spec((DEPTH, 2, BRANCH))],
        out_specs=[_tok_spec(tm, BRANCH)] * 5,
        out_shape=[bf, f32, f32, bf, bf],
        compiler_params=_cparams(("parallel", "parallel")),
        name="hg_in",
    )(x, g, w, lb_raw)


def _hg_level_operands(q, k, lf2, cum, n, fwd, sel):
    L = HG_CHUNK
    half = n // 2
    ref_row = half - 1 if fwd else half
    if n >= 16:
        zero = jnp.zeros((half, HG_DK), BF16)
        qt_parts, kt_parts = [], []
        for blk in range(L // n):
            cm = cum[blk * n + ref_row:blk * n + ref_row + 1]
            for is_upper in (False, True):
                r0 = blk * n + (half if is_upper else 0)
                seg = slice(r0, r0 + half)
                e = jnp.exp2(-jnp.abs(cum[seg] - cm))
                if is_upper == fwd:
                    qt_parts.append((q[seg] * e).astype(BF16))
                    kt_parts.append(zero)
                else:
                    qt_parts.append(zero)
                    kt_parts.append((k[seg] * e).astype(BF16))
        return jnp.concatenate(qt_parts, axis=0), jnp.concatenate(kt_parts, axis=0)
    q_side = sel["upper"][n] if fwd else jnp.logical_not(sel["upper"][n])
    if n == 2:
        gl = jnp.where(q_side, lf2, 0.0)
    else:
        c3 = cum.reshape(L // n, n, HG_DK)
        cm = jnp.broadcast_to(c3[:, ref_row:ref_row + 1, :], c3.shape).reshape(L, HG_DK)
        gl = -jnp.abs(cum - cm)
    e = jnp.exp2(gl)
    return jnp.where(q_side, q * e, 0.0).astype(BF16), jnp.where(q_side, 0.0, k * e).astype(BF16)


def _hg_chunks(items, sel):
    L = HG_CHUNK
    cats = []
    for q, lf2, v, tri3, fwd in items:
        hi = lf2.astype(BF16)
        r1 = lf2 - hi.astype(F32)
        mid = r1.astype(BF16)
        lo = (r1 - mid.astype(F32)).astype(BF16)
        cats.append(jnp.concatenate([hi, mid, lo], axis=0))
    cums = [_dot(it[3], cat) for it, cat in zip(items, cats)]
    ks = [1.0 - jnp.exp2(it[1]) for it in items]
    accs = [jnp.where(sel["eye"], jnp.sum(it[0] * k, axis=-1, keepdims=True), 0.0)
            for it, k in zip(items, ks)]
    for n in HG_LEVELS:
        ops = [_hg_level_operands(it[0], k, it[1], cum, n, it[4], sel)
               for it, k, cum in zip(items, ks, cums)]
        prods = [_dot_nt(qt, kt) for qt, kt in ops]
        accs = [jnp.where(sel["own"][n], p, a) for p, a in zip(prods, accs)]
    o_intras = [_dot(a.astype(BF16), it[2]) for a, it in zip(accs, items)]
    edges = [cum[L - 1:L] if it[4] else cum[0:1] for it, cum in zip(items, cums)]
    khs = [(k * jnp.exp2(e - cum)).astype(BF16) for k, e, cum in zip(ks, edges, cums)]
    us = [_dot_tn(it[2], kh) for it, kh in zip(items, khs)]
    qbs = [(it[0] * jnp.exp2(cum)).astype(BF16) for it, cum in zip(items, cums)]
    return [(o, qb, u, jnp.exp2(e)) for o, qb, u, e in zip(o_intras, qbs, us, edges)]


def _hg_scan_kernel(q_ref, lf_ref, lb_ref, v_ref, z_ref, g_ref, o_ref,
                    of_ref, ob_ref, sf_ref, sb_ref, *, nchunks, fin_rows):
    L = HG_CHUNK
    t_i = lax.broadcasted_iota(jnp.int32, (L, L), 0)
    s_i = lax.broadcasted_iota(jnp.int32, (L, L), 1)
    x_i = t_i ^ s_i
    row = lax.broadcasted_iota(jnp.int32, (L, HG_DK), 0)
    sel = {"eye": t_i == s_i,
           "own": {n: (x_i < n) & (x_i >= n // 2) for n in HG_LEVELS},
           "upper": {n: (row & (n - 1)) >= n // 2 for n in HG_LEVELS if n < 16}}
    t3 = lax.broadcasted_iota(jnp.int32, (L, 3 * L), 0)
    u3 = lax.broadcasted_iota(jnp.int32, (L, 3 * L), 1) & (L - 1)
    tri_f = (u3 <= t3).astype(BF16)
    tri_b = (u3 >= t3).astype(BF16)
    sf_ref[...] = jnp.zeros_like(sf_ref)
    sb_ref[...] = jnp.zeros_like(sb_ref)

    def body(i, carry):
        dirs = ((True, lf_ref, sf_ref, of_ref, tri_f), (False, lb_ref, sb_ref, ob_ref, tri_b))
        items, rows_of = [], []
        for fwd, gate_ref, _, _, tri in dirs:
            for jj in range(HG_UNROLL):
                c = i * HG_UNROLL + jj
                c = c if fwd else nchunks - 1 - c
                rows = pl.ds(pl.multiple_of(c * L, L), L)
                rows_of.append(rows)
                items.append((q_ref[rows, :].astype(F32), gate_ref[rows, :], v_ref[rows, :], tri, fwd))
        parts = _hg_chunks(items, sel)
        sts = [d[2][...] for d in dirs]
        for jj in range(HG_UNROLL):
            for di, (_, _, _, out_ref, _) in enumerate(dirs):
                o_intra, qb, u, dec = parts[di * HG_UNROLL + jj]
                out_ref[rows_of[di * HG_UNROLL + jj], :] = o_intra + _dot_nt(qb, sts[di].astype(BF16))
                sts[di] = sts[di] * dec + u
        for d, st in zip(dirs, sts):
            d[2][...] = st
        return carry

    lax.fori_loop(0, nchunks // HG_UNROLL, body, 0)

    def fin(i, carry):
        rows = pl.ds(pl.multiple_of(i * fin_rows, fin_rows), fin_rows)
        o = of_ref[rows, :] + ob_ref[rows, :]
        y = _rmsnorm(o, g_ref[...])
        o_ref[rows, :] = (y * _silu(z_ref[rows, :].astype(F32))).astype(BF16)
        return carry

    lax.fori_loop(0, (nchunks * L) // fin_rows, fin, 0)


def _hg_scan(q, lf, lb, v, z, g_out):
    b, s, _ = q.shape
    head = pl.BlockSpec((None, s, HG_DK), lambda bb, h: (bb, 0, h))
    fin_rows = min(256, s)
    return pl.pallas_call(
        functools.partial(_hg_scan_kernel, nchunks=s // HG_CHUNK, fin_rows=fin_rows),
        grid=(b, HG_HEADS),
        in_specs=[head] * 5 + [pl.BlockSpec((1, HG_DK), lambda bb, h: (0, h))],
        out_specs=head,
        out_shape=jax.ShapeDtypeStruct((b, s, BRANCH), BF16),
        scratch_shapes=[pltpu.VMEM((s, HG_DK), F32), pltpu.VMEM((s, HG_DK), F32),
                        pltpu.VMEM((HG_DK, HG_DK), F32), pltpu.VMEM((HG_DK, HG_DK), F32)],
        compiler_params=_cparams(("parallel", "parallel")),
        name="hg_scan",
    )(q, lf, lb, v, z, g_out)


def _rot_half_cols(w):
    half = MLA_ROPE // 2
    return jnp.concatenate([-w[..., half:], w[..., :half]], axis=-1)


def _prep(s, norm_g, fn_w_in, fn_w_mix, fn_w_out, na_w_in, na_rpb, na_w_out,
          mla_w_in, mla_g_q, mla_w_uq, mla_g_kv, mla_w_ukv, mla_w_out,
          hg_w_in, hg_lb_raw, hg_g_out, hg_w_out, ple_w, ple_gate_w, final_g, tk):
    bf = lambda t: t.astype(BF16)
    row = lambda t: t.reshape(1, -1).astype(F32)
    w = {}
    w["norm_g"] = [row(norm_g[i]) for i in range(DEPTH)]
    w["final_g"] = row(final_g)
    w["ple_w"] = [bf(ple_w[i]) for i in range(DEPTH)]
    w["gate_w"] = [bf(ple_gate_w[i]) for i in range(DEPTH)]
    w["w_out"] = [bf(fn_w_out[0]), bf(na_w_out[0]), bf(mla_w_out[0]), bf(hg_w_out[0])]
    w["fn_w_in"] = bf(fn_w_in[0])
    w["fn_w2"] = _fn_fold(fn_w_mix[0].astype(F32), s)
    w["dft"] = _dft_mats(s, tk)
    w["na_w_in"] = bf(na_w_in[0])
    w["na_wkt"] = bf(na_w_in[0][:, BRANCH:2 * BRANCH].T)
    w["na_table"] = _na_bias_table(na_rpb[0].astype(F32))
    wi = mla_w_in[0]
    c0, c1, c2 = MLA_Q_LORA, MLA_Q_LORA + MLA_KV_LORA, MLA_Q_LORA + MLA_KV_LORA + MLA_ROPE
    wkpe = wi[:, c1:c2]
    wrot = _rot_half_cols(wkpe)
    w["mla_w1"] = bf(jnp.concatenate([wi[:, :c1], wkpe, wkpe, wrot, wrot, wi[:, c2:]], axis=1))
    wq = mla_w_uq[0].reshape(MLA_Q_LORA, MLA_HEADS, MLA_NOPE + MLA_ROPE)
    wq_pe = wq[..., MLA_NOPE:]
    w["mla_wuq"] = bf(jnp.concatenate([wq[..., :MLA_NOPE], wq_pe, _rot_half_cols(wq_pe)], axis=-1)
                      .reshape(MLA_Q_LORA, MLA_HEADS * MLA_QK))
    wkv = mla_w_ukv[0].reshape(MLA_KV_LORA, MLA_HEADS, MLA_NOPE + MLA_V)
    w["mla_wuk"] = bf(wkv[..., :MLA_NOPE].reshape(MLA_KV_LORA, BRANCH))
    w["mla_wuvt"] = bf(wkv[..., MLA_NOPE:].reshape(MLA_KV_LORA, BRANCH).T)
    w["mla_gq"] = row(mla_g_q[0])
    w["mla_gkv"] = row(mla_g_kv[0])
    half = MLA_ROPE // 2
    inv = ROPE_THETA ** (-jnp.arange(half, dtype=F32) / half)
    ang = jnp.arange(s, dtype=F32)[:, None] * inv[None, :]
    cos, sin = jnp.cos(ang), jnp.sin(ang)
    cos64 = jnp.concatenate([cos, cos], axis=-1)
    sin64 = jnp.concatenate([sin, sin], axis=-1)
    scale = (MLA_NOPE + MLA_ROPE) ** -0.5 * math.log2(math.e)
    w["mla_rq"] = scale * jnp.concatenate([jnp.ones((s, MLA_NOPE), F32), cos64, sin64], axis=-1)
    w["mla_rk"] = jnp.concatenate([cos64, cos64, sin64, sin64], axis=-1)
    w["hg_w_in"] = bf(hg_w_in[0])
    w["hg_lb_raw"] = hg_lb_raw.astype(F32)
    w["hg_g_out"] = row(hg_g_out[0])
    return w


def _trunk(x, p, w, tm, tk, tq, tkv):
    def finish(x, o, li, final=False):
        return _out_proj(x, o, p, li, w["w_out"][li], w["gate_w"][li], w["ple_w"][li],
                         w["final_g"] if final else None, tm)

    u, z = _fn_in(x, w["norm_g"][0], w["fn_w_in"], tm)
    o = _fn_dft(w["dft"], u, z, w["fn_w2"], tk)
    x = finish(x, o, 0)
    q, kt, v, z = _na_in(x, w["norm_g"][1], w["na_w_in"], w["na_wkt"], tm)
    kt_shift = jnp.concatenate([kt[:, :, GRID_W:], jnp.zeros_like(kt[:, :, :GRID_W])], axis=2)
    o = _na_attn(q, jnp.stack([kt, kt_shift], axis=1), v, z, w["na_table"])
    x = finish(x, o, 1)
    q, k, vt, z = _mla_in(x, w["norm_g"][2], w["mla_w1"], w["mla_gq"], w["mla_wuq"], w["mla_gkv"],
                          w["mla_wuk"], w["mla_wuvt"], w["mla_rq"], w["mla_rk"], tm)
    o = _mla_attn(q, k, vt, z, tq, tkv)
    x = finish(x, o, 2)
    q, lf, lb, v, z = _hg_in(x, w["norm_g"][3], w["hg_w_in"], w["hg_lb_raw"], 3, tm)
    o = _hg_scan(q, lf, lb, v, z, w["hg_g_out"])
    return finish(x, o, 3, final=True)


def kernel(x_prompt, x_sample, p_prompt, p_sample, norm_g, fn_w_in, fn_w_mix, fn_w_out, na_w_in, na_rpb, na_w_out, mla_w_in, mla_g_q, mla_w_uq, mla_g_kv, mla_w_ukv, mla_w_out, hg_w_in, hg_lb_raw, hg_g_out, hg_w_out, ple_w, ple_gate_w, final_g):
    s = x_prompt.shape[1]
    assert x_sample.shape[1] == s
    tm = min(512, s)
    tk = min(512, s)
    tq = min(1024, s)
    tkv = min(1024, s)
    w = _prep(s, norm_g, fn_w_in, fn_w_mix, fn_w_out, na_w_in, na_rpb, na_w_out,
              mla_w_in, mla_g_q, mla_w_uq, mla_g_kv, mla_w_ukv, mla_w_out,
              hg_w_in, hg_lb_raw, hg_g_out, hg_w_out, ple_w, ple_gate_w, final_g, tk)
    y_prompt = _trunk(x_prompt, p_prompt, w, tm, tk, tq, tkv)
    y_sample = _trunk(x_sample, p_sample, w, tm, tk, tq, tkv)
    return (y_prompt, y_sample)
```

```python
import functools
import math

import numpy as np
import jax
import jax.numpy as jnp
from jax import lax
from jax.experimental import pallas as pl
from jax.experimental.pallas import tpu as pltpu

F32 = jnp.float32
BF16 = jnp.bfloat16

D_MODEL = 1024
DEPTH = 4
BRANCH = 1024
PLE_DIM = 256
GRID_W = 64
EPS = 1e-6
LANE = 128
VMEM_LIMIT = 56 * 1024 * 1024

FN_GROUPS = 8
FN_GDIM = BRANCH // FN_GROUPS

NA_HEADS = 32
NA_HDIM = BRANCH // NA_HEADS
NA_KH = 8
NA_KW = 16
NA_HPG = LANE // NA_HDIM
NA_GROUPS = NA_HEADS // NA_HPG
NA_NEG = -1e30

MLA_HEADS = 8
MLA_NOPE = 128
MLA_ROPE = 64
MLA_V = BRANCH // MLA_HEADS
MLA_Q_LORA = 384
MLA_KV_LORA = 256
MLA_QK = MLA_NOPE + 2 * MLA_ROPE
ROPE_THETA = 10000.0

HG_HEADS = 8
HG_DK = BRANCH // HG_HEADS
HG_CHUNK = 64
HG_LEVELS = (64, 32, 16, 8, 4, 2)
HG_UNROLL = 4
NA_UNROLL = 8
LOG2E = math.log2(math.e)


def _cparams(sem):
    return pltpu.CompilerParams(dimension_semantics=sem, vmem_limit_bytes=VMEM_LIMIT)


def _rmsnorm(x, g):
    ms = jnp.mean(x * x, axis=-1, keepdims=True)
    return x * lax.rsqrt(ms + EPS) * g


def _sigmoid(x):
    return 1.0 / (1.0 + jnp.exp(-x))


def _silu(x):
    return x * _sigmoid(x)


def _dot(a, b):
    return jnp.dot(a, b, preferred_element_type=F32)


def _dot_nt(a, b):
    return lax.dot_general(a, b, (((1,), (1,)), ((), ())), preferred_element_type=F32)


def _dot_tn(a, b):
    return lax.dot_general(a, b, (((0,), (0,)), ((), ())), preferred_element_type=F32)


def _tok_spec(tm, width):
    return pl.BlockSpec((None, tm, width), lambda b, i: (b, i, 0))


def _full_spec(shape):
    nd = len(shape)
    return pl.BlockSpec(shape, lambda *_: (0,) * nd)


def _fn_in_kernel(x_ref, g_ref, w_ref, u_ref, z_ref):
    h = _rmsnorm(x_ref[...], g_ref[...]).astype(BF16)
    u_ref[...] = _dot(h, w_ref[:, :BRANCH]).astype(BF16)
    z_ref[...] = _dot(h, w_ref[:, BRANCH:]).astype(BF16)


def _fn_in(x, g, w, tm):
    b, s, d = x.shape
    return pl.pallas_call(
        _fn_in_kernel,
        grid=(b, s // tm),
        in_specs=[_tok_spec(tm, d), _full_spec((1, d)), _full_spec((d, 2 * BRANCH))],
        out_specs=[_tok_spec(tm, BRANCH)] * 2,
        out_shape=[jax.ShapeDtypeStruct((b, s, BRANCH), BF16)] * 2,
        compiler_params=_cparams(("parallel", "parallel")),
        name="fn_in",
    )(x, g, w)


def _fn_fold_kernel(c_ref, s_ref, w_ref, o_ref, *, scale):
    w = w_ref[...]
    hi = lax.Precision.HIGHEST
    wc = (scale * jnp.dot(c_ref[...], w, precision=hi, preferred_element_type=F32)).astype(BF16)
    ws = (scale * jnp.dot(s_ref[...], w, precision=hi, preferred_element_type=F32)).astype(BF16)
    o_ref[:FN_GDIM, :FN_GDIM] = wc
    o_ref[:FN_GDIM, FN_GDIM:] = wc
    o_ref[FN_GDIM:, :FN_GDIM] = -ws
    o_ref[FN_GDIM:, FN_GDIM:] = ws


def _fn_fold(w_mix, s):
    c = jnp.arange(FN_GDIM, dtype=jnp.int32)
    ang = ((c[:, None] * c[None, :]) % FN_GDIM).astype(F32) * (2.0 * math.pi / FN_GDIM)
    scale = 1.0 / math.sqrt(s * FN_GDIM)
    gspec = pl.BlockSpec((None, FN_GDIM, FN_GDIM), lambda g: (g, 0, 0))
    return pl.pallas_call(
        functools.partial(_fn_fold_kernel, scale=scale),
        grid=(FN_GROUPS,),
        in_specs=[_full_spec((FN_GDIM, FN_GDIM))] * 2 + [gspec],
        out_specs=pl.BlockSpec((None, 2 * FN_GDIM, 2 * FN_GDIM), lambda g: (g, 0, 0)),
        out_shape=jax.ShapeDtypeStruct((FN_GROUPS, 2 * FN_GDIM, 2 * FN_GDIM), BF16),
        name="fn_fold",
    )(jnp.cos(ang), jnp.sin(ang), w_mix)


def _dft_mats(s, tk):
    nt = s // 2 // tk
    k = jnp.arange(s // 2, dtype=jnp.int32)
    pos = jnp.arange(s, dtype=jnp.int32)
    ang = ((k[:, None] * pos[None, :]) % s).astype(F32) * (2.0 * math.pi / s)
    c = jnp.cos(ang).astype(BF16).reshape(nt, tk, s)
    sn = jnp.sin(ang).astype(BF16).reshape(nt, tk, s)
    nyq = jnp.where(pos % 2 == 0, 1.0, -1.0).astype(BF16)
    e = jnp.concatenate([nyq[None, :], jnp.zeros((7, s), BF16)], axis=0)
    r = np.arange(tk)
    rev = np.zeros((tk, tk), np.float32)
    rev[r[1:], tk - r[1:]] = 1.0
    return jnp.concatenate([c, sn], axis=1), e, jnp.asarray(rev, BF16)


def _fn_dft_kernel(m_ref, e_ref, rev_ref, u_ref, zd_ref, zm_ref, w2_ref, o_ref, r_ref, carry_ref, *, tk):
    u = u_ref[...]

    @pl.when(pl.program_id(1) == 0)
    def _():
        ev = _dot(e_ref[...], u).astype(BF16)
        for g in range(FN_GROUPS):
            sl = slice(g * FN_GDIM, (g + 1) * FN_GDIM)
            ab = jnp.concatenate([ev[:, sl], jnp.zeros_like(ev[:, sl])], axis=1)
            carry_ref[:, sl] = _dot(ab, w2_ref[g])[:, FN_GDIM:]

    r = _dot(m_ref[...], u)
    a = r[:tk].astype(BF16)
    bm = r[tk:].astype(BF16)
    carry = carry_ref[0:1, :]
    for g in range(FN_GROUPS):
        sl = slice(g * FN_GDIM, (g + 1) * FN_GDIM)
        ab = jnp.concatenate([a[:, sl], bm[:, sl]], axis=1)
        y2 = _dot(ab, w2_ref[g])
        o_ref[0, :, sl] = (y2[:, :FN_GDIM] * _silu(zd_ref[:, sl].astype(F32))).astype(BF16)
        r_ref[:, sl] = y2[:, FN_GDIM:].astype(BF16)
        carry_ref[:, sl] = y2[0:8, FN_GDIM:]
    mirror = _dot(rev_ref[...], r_ref[...])
    row0 = lax.broadcasted_iota(jnp.int32, (tk, BRANCH), 0) == 0
    mirror = jnp.where(row0, carry, mirror)
    o_ref[1] = (mirror * _silu(zm_ref[...].astype(F32))).astype(BF16)


def _fn_dft(mats, u, z, w2, tk):
    m, e, rev = mats
    b, s, _ = u.shape
    nt = s // 2 // tk
    return pl.pallas_call(
        functools.partial(_fn_dft_kernel, tk=tk),
        grid=(b, nt),
        in_specs=[pl.BlockSpec((None, 2 * tk, s), lambda bb, jj: (nt - 1 - jj, 0, 0)),
                  _full_spec((8, s)), _full_spec((tk, tk)),
                  pl.BlockSpec((None, s, BRANCH), lambda bb, jj: (bb, 0, 0)),
                  pl.BlockSpec((None, tk, BRANCH), lambda bb, jj: (bb, nt - 1 - jj, 0)),
                  pl.BlockSpec((None, tk, BRANCH), lambda bb, jj: (bb, nt + jj, 0)),
                  _full_spec((FN_GROUPS, 2 * FN_GDIM, 2 * FN_GDIM))],
        out_specs=pl.BlockSpec((None, 2, tk, BRANCH), lambda bb, jj: (bb, 0, nt - 1 - jj, 0)),
        out_shape=jax.ShapeDtypeStruct((b, 2, s // 2, BRANCH), BF16),
        scratch_shapes=[pltpu.VMEM((tk, BRANCH), BF16), pltpu.VMEM((8, BRANCH), F32)],
        compiler_params=_cparams(("parallel", "arbitrary")),
        name="fn_dft",
    )(m, e, rev, u, z, z, w2)


def _out_kernel(x_ref, o_ref, p_ref, wo_ref, gw_ref, pw_ref, *rest, final):
    y_ref = rest[-1]
    x1 = x_ref[...] + _dot(o_ref[...], wo_ref[...])
    gate = _sigmoid(_dot(x1.astype(BF16), gw_ref[...]))
    x2 = x1 + gate * _dot(p_ref[...].astype(BF16), pw_ref[...])
    if final:
        x2 = _rmsnorm(x2, rest[0][...])
    y_ref[...] = x2


def _out_proj(x, o, p, li, wo, gw, pw, final_g, tm):
    b, s, d = x.shape
    final = final_g is not None
    if o.ndim == 4:
        nt = s // 2 // tm
        assert o.shape == (b, 2, nt * tm, BRANCH)
        o_spec = pl.BlockSpec((None, None, tm, BRANCH),
                              lambda bb, i: (bb, i // nt, jnp.where(i < nt, i, 2 * nt - 1 - i), 0))
    else:
        o_spec = _tok_spec(tm, BRANCH)
    in_specs = [_tok_spec(tm, d), o_spec,
                pl.BlockSpec((None, None, tm, PLE_DIM), lambda bb, i: (li, bb, i, 0)),
                _full_spec((BRANCH, d)), _full_spec((d, d)), _full_spec((PLE_DIM, d))]
    args = [x, o, p, wo, gw, pw]
    if final:
        in_specs.append(_full_spec((1, d)))
        args.append(final_g)
    return pl.pallas_call(
        functools.partial(_out_kernel, final=final),
        grid=(b, s // tm),
        in_specs=in_specs,
        out_specs=_tok_spec(tm, d),
        out_shape=jax.ShapeDtypeStruct((b, s, d), F32),
        compiler_params=_cparams(("parallel", "parallel")),
        name="out_proj",
    )(*args)


def _na_in_kernel(x_ref, g_ref, w_ref, q_ref, k_ref, v_ref, z_ref):
    h = _rmsnorm(x_ref[...], g_ref[...]).astype(BF16)
    q_ref[...] = (_dot(h, w_ref[:, :BRANCH]) * (NA_HDIM ** -0.5 * LOG2E)).astype(BF16)
    k_ref[...] = _dot(h, w_ref[:, BRANCH:2 * BRANCH]).astype(BF16)
    v_ref[...] = _dot(h, w_ref[:, 2 * BRANCH:3 * BRANCH]).astype(BF16)
    z_ref[...] = _dot(h, w_ref[:, 3 * BRANCH:]).astype(BF16)


def _na_in(x, g, w, tm):
    b, s, d = x.shape
    tok = jax.ShapeDtypeStruct((b, s, BRANCH), BF16)
    return pl.pallas_call(
        _na_in_kernel,
        grid=(b, s // tm),
        in_specs=[_tok_spec(tm, d), _full_spec((1, d)), _full_spec((d, 4 * BRANCH))],
        out_specs=[_tok_spec(tm, BRANCH)] * 4,
        out_shape=[tok] * 4,
        compiler_params=_cparams(("parallel", "parallel")),
        name="na_in",
    )(x, g, w)


def _na_bias_table(rpb):
    d = np.arange(NA_KH)
    i = np.arange(NA_KH)
    ro = i[None, :] - d[:, None] + (NA_KH - 1)
    c = np.arange(GRID_W)
    kc = np.arange(GRID_W)
    win = np.clip(c - NA_KW // 2, 0, GRID_W - NA_KW)
    valid = (kc[:, None] >= win[None, :]) & (kc[:, None] < win[None, :] + NA_KW)
    pad = GRID_W - NA_KW
    r1 = jnp.pad(rpb[:, ro, :] * LOG2E, ((0, 0), (0, 0), (0, 0), (pad, pad)))
    cols = [r1[..., GRID_W - 1 - cc:2 * GRID_W - 1 - cc] for cc in range(GRID_W)]
    bias = jnp.stack(cols, axis=-1)
    bias = jnp.where(jnp.asarray(valid)[None, None, None], bias, NA_NEG)
    bias = bias.reshape(NA_GROUPS, NA_HPG, NA_KH, NA_KH, GRID_W, GRID_W)
    bias = jnp.transpose(bias, (0, 2, 3, 4, 1, 5))
    return bias.reshape(NA_GROUPS, NA_KH, NA_KH * GRID_W, NA_HPG * GRID_W).astype(F32)


def _na_attn_kernel(q_ref, k_ref, v_ref, z_ref, t_ref, o_ref, ot_ref, *, rows):
    lane = lax.broadcasted_iota(jnp.int32, (GRID_W, LANE), 1)
    head_masks = [(lane >= hh * NA_HDIM) & (lane < (hh + 1) * NA_HDIM) for hh in range(NA_HPG)]
    row_head = lax.broadcasted_iota(jnp.int32, (LANE, LANE), 0) // NA_HDIM
    low_heads = row_head < NA_HPG // 2
    even_heads = (row_head & 1) == 0
    nkeys = NA_KH * GRID_W

    def logits(r):
        rs = jnp.clip(r - NA_KH // 2, 0, rows - NA_KH)
        q_r = q_ref[pl.ds(pl.multiple_of(r * GRID_W, GRID_W), GRID_W), :]
        zero = jnp.zeros_like(q_r)
        qs = jnp.concatenate([jnp.where(m, q_r, zero) for m in head_masks], axis=0)
        kwin = k_ref[pl.ds(pl.multiple_of(rs * GRID_W, GRID_W), nkeys), :]
        return _dot_nt(kwin, qs) + t_ref[r - rs]

    def probs(sc):
        p = jnp.exp2(sc - jnp.max(sc, axis=0, keepdims=True))
        return p.astype(BF16), jnp.sum(p, axis=0, keepdims=True)

    def values(r, p, l):
        rs = jnp.clip(r - NA_KH // 2, 0, rows - NA_KH)
        vwin = v_ref[pl.ds(pl.multiple_of(rs * GRID_W, GRID_W), nkeys), :]
        return _dot_tn(vwin, p) * (1.0 / l)

    def emit(r, ot):
        w = jnp.where(low_heads, ot[:, :LANE], ot[:, LANE:])
        w = jnp.where(even_heads, w, pltpu.roll(w, GRID_W, axis=1))
        o = w.T[:GRID_W]
        q0 = pl.multiple_of(r * GRID_W, GRID_W)
        zz = z_ref[pl.ds(q0, GRID_W), :].astype(F32)
        o_ref[pl.ds(q0, GRID_W), :] = (o * _silu(zz)).astype(BF16)

    def step(i, emit_previous):
        rws = [i * NA_UNROLL + u for u in range(NA_UNROLL)]
        scs = [logits(r) for r in rws]
        if emit_previous:
            for u in range(NA_UNROLL):
                emit((i - 1) * NA_UNROLL + u, ot_ref[u])
        pls = [probs(sc) for sc in scs]
        for u, (r, (p, l)) in enumerate(zip(rws, pls)):
            ot_ref[u] = values(r, p, l)

    def body(i, carry):
        step(i, True)
        return carry

    nsteps = rows // NA_UNROLL
    step(0, False)
    lax.fori_loop(1, nsteps, body, 0)
    for u in range(NA_UNROLL):
        emit((nsteps - 1) * NA_UNROLL + u, ot_ref[u])


def _na_attn(q, k, v, z, table):
    b, s, _ = q.shape
    rows = s // GRID_W
    assert rows >= NA_KH
    grp = pl.BlockSpec((None, s, LANE), lambda g, bb: (bb, 0, g))
    return pl.pallas_call(
        functools.partial(_na_attn_kernel, rows=rows),
        grid=(NA_GROUPS, b),
        in_specs=[grp, grp, grp, grp,
                  pl.BlockSpec((None, NA_KH, NA_KH * GRID_W, NA_HPG * GRID_W),
                               lambda g, bb: (g, 0, 0, 0))],
        out_specs=grp,
        out_shape=jax.ShapeDtypeStruct((b, s, BRANCH), BF16),
        scratch_shapes=[pltpu.VMEM((NA_UNROLL, LANE, NA_HPG * GRID_W), F32)],
        compiler_params=_cparams(("parallel", "parallel")),
        name="na_attn",
    )(q, k, v, z, table)


def _mla_in_kernel(x_ref, g_ref, w1_ref, gq_ref, wuq_ref, gkv_ref, wuk_ref, wuvt_ref,
                   rq_ref, rk_ref, q_ref, k_ref, vt_ref, z_ref):
    h = _rmsnorm(x_ref[...], g_ref[...]).astype(BF16)
    c0, c1, c2, c3 = MLA_Q_LORA, MLA_Q_LORA + MLA_KV_LORA, MLA_Q_LORA + MLA_KV_LORA + LANE, \
        MLA_Q_LORA + MLA_KV_LORA + 2 * LANE
    cq = _dot(h, w1_ref[:, :c0])
    ckv = _dot(h, w1_ref[:, c0:c1])
    kp = _dot(h, w1_ref[:, c1:c2])
    kr = _dot(h, w1_ref[:, c2:c3])
    z_ref[...] = _dot(h, w1_ref[:, c3:]).astype(BF16)
    cqn = _rmsnorm(cq, gq_ref[...]).astype(BF16)
    ckvn = _rmsnorm(ckv, gkv_ref[...]).astype(BF16)
    rq = rq_ref[...]
    for hh in range(MLA_HEADS):
        sl = slice(hh * MLA_QK, (hh + 1) * MLA_QK)
        q_ref[:, sl] = (_dot(cqn, wuq_ref[:, sl]) * rq).astype(BF16)
    krr = (kp * rk_ref[:, :LANE] + kr * rk_ref[:, LANE:]).astype(BF16)
    kn = _dot(ckvn, wuk_ref[...]).astype(BF16)
    for hh in range(MLA_HEADS):
        k_ref[:, hh * MLA_QK:hh * MLA_QK + MLA_NOPE] = kn[:, hh * MLA_NOPE:(hh + 1) * MLA_NOPE]
        k_ref[:, hh * MLA_QK + MLA_NOPE:(hh + 1) * MLA_QK] = krr
    vt_ref[...] = _dot_nt(wuvt_ref[...], ckvn).astype(BF16)


def _mla_in(x, g, w1, gq, wuq, gkv, wuk, wuvt, rq, rk, tm):
    b, s, d = x.shape
    n1 = w1.shape[1]
    qk = MLA_HEADS * MLA_QK
    rope_spec = pl.BlockSpec((tm, 2 * LANE), lambda bb, i: (i, 0))
    return pl.pallas_call(
        _mla_in_kernel,
        grid=(b, s // tm),
        in_specs=[_tok_spec(tm, d), _full_spec((1, d)), _full_spec((d, n1)),
                  _full_spec((1, MLA_Q_LORA)), _full_spec((MLA_Q_LORA, qk)),
                  _full_spec((1, MLA_KV_LORA)), _full_spec((MLA_KV_LORA, BRANCH)),
                  _full_spec((BRANCH, MLA_KV_LORA)), rope_spec, rope_spec],
        out_specs=[_tok_spec(tm, qk), _tok_spec(tm, qk),
                   pl.BlockSpec((None, BRANCH, tm), lambda bb, i: (bb, 0, i)), _tok_spec(tm, BRANCH)],
        out_shape=[jax.ShapeDtypeStruct((b, s, qk), BF16), jax.ShapeDtypeStruct((b, s, qk), BF16),
                   jax.ShapeDtypeStruct((b, BRANCH, s), BF16), jax.ShapeDtypeStruct((b, s, BRANCH), BF16)],
        compiler_params=_cparams(("parallel", "parallel")),
        name="mla_in",
    )(x, g, w1, gq, wuq, gkv, wuk, wuvt, rq, rk)


def _mla_attn_kernel(q_ref, k_ref, vt_ref, z_ref, o_ref, *, tk, nk):
    q = q_ref[...]
    tq = q.shape[0]
    m = jnp.full((1, tq), -jnp.inf, F32)
    l = jnp.zeros((1, tq), F32)
    acc = jnp.zeros((MLA_V, tq), F32)
    logits = lambda j: _dot_nt(k_ref[j * tk:(j + 1) * tk, :], q)
    st_next = logits(0)
    for j in range(nk):
        st = st_next
        if j + 1 < nk:
            st_next = logits(j + 1)
        m_new = jnp.maximum(m, jnp.max(st, axis=0, keepdims=True))
        alpha = jnp.exp2(m - m_new)
        p = jnp.exp2(st - m_new)
        l = alpha * l + jnp.sum(p, axis=0, keepdims=True)
        acc = alpha * acc + _dot(vt_ref[:, j * tk:(j + 1) * tk], p.astype(BF16))
        m = m_new
    o = (acc * (1.0 / l)).T
    o_ref[...] = (o * _silu(z_ref[...].astype(F32))).astype(BF16)


def _mla_attn(q, k, vt, z, tq, tk):
    b, _, s = vt.shape
    return pl.pallas_call(
        functools.partial(_mla_attn_kernel, tk=tk, nk=s // tk),
        grid=(b, MLA_HEADS, s // tq),
        in_specs=[pl.BlockSpec((None, tq, MLA_QK), lambda bb, h, i: (bb, i, h)),
                  pl.BlockSpec((None, s, MLA_QK), lambda bb, h, i: (bb, 0, h)),
                  pl.BlockSpec((None, MLA_V, s), lambda bb, h, i: (bb, h, 0)),
                  pl.BlockSpec((None, tq, MLA_V), lambda bb, h, i: (bb, i, h))],
        out_specs=pl.BlockSpec((None, tq, MLA_V), lambda bb, h, i: (bb, i, h)),
        out_shape=jax.ShapeDtypeStruct((b, s, BRANCH), BF16),
        compiler_params=_cparams(("parallel", "parallel", "arbitrary")),
        name="mla_attn",
    )(q, k, vt, z)


def _hg_in_kernel(x_ref, g_ref, w_ref, lbr_ref, q_ref, lf_ref, lb_ref, i_ref, z_ref, *, li):
    h = _rmsnorm(x_ref[...], g_ref[...]).astype(BF16)
    raw = lbr_ref[...]
    e = jnp.exp(raw - jnp.max(raw, axis=0, keepdims=True))
    sm = e / jnp.sum(e, axis=0, keepdims=True)
    lower = jnp.zeros_like(sm[0])
    for dd in range(1, li + 1):
        lower = lower + sm[dd]
    q_ref[...] = _silu(_dot(h, w_ref[:, :BRANCH])).astype(BF16)
    for idx, ref in ((0, lf_ref), (1, lb_ref)):
        lb = lower[idx:idx + 1]
        raw_f = _dot(h, w_ref[:, (1 + idx) * BRANCH:(2 + idx) * BRANCH])
        ref[...] = jnp.log(lb + (1.0 - lb) * _sigmoid(raw_f)) * LOG2E
    i_ref[...] = _dot(h, w_ref[:, 3 * BRANCH:4 * BRANCH]).astype(BF16)
    z_ref[...] = _dot(h, w_ref[:, 4 * BRANCH:]).astype(BF16)


def _hg_in(x, g, w, lb_raw, li, tm):
    b, s, d = x.shape
    bf = jax.ShapeDtypeStruct((b, s, BRANCH), BF16)
    f32 = jax.ShapeDtypeStruct((b, s, BRANCH), F32)
    return pl.pallas_call(
        functools.partial(_hg_in_kernel, li=li),
        grid=(b, s // tm),
        in_specs=[_tok_spec(tm, d), _full_spec((1, d)), _full_spec((d, 5 * BRANCH)),
                  _full_spec((DEPTH, 2, BRANCH))],
        out_specs=[_tok_spec(tm, BRANCH)] * 5,
        out_shape=[bf, f32, f32, bf, bf],
        compiler_params=_cparams(("parallel", "parallel")),
        name="hg_in",
    )(x, g, w, lb_raw)


def _hg_level_operands(q, k, lf2, cum, n, fwd, sel):
    L = HG_CHUNK
    half = n // 2
    ref_row = half - 1 if fwd else half
    if n >= 16:
        zero = jnp.zeros((half, HG_DK), BF16)
        qt_parts, kt_parts = [], []
        for blk in range(L // n):
            cm = cum[blk * n + ref_row:blk * n + ref_row + 1]
            for is_upper in (False, True):
                r0 = blk * n + (half if is_upper else 0)
                seg = slice(r0, r0 + half)
                e = jnp.exp2(-jnp.abs(cum[seg] - cm))
                if is_upper == fwd:
                    qt_parts.append((q[seg] * e).astype(BF16))
                    kt_parts.append(zero)
                else:
                    qt_parts.append(zero)
                    kt_parts.append((k[seg] * e).astype(BF16))
        return jnp.concatenate(qt_parts, axis=0), jnp.concatenate(kt_parts, axis=0)
    if n == 2:
        q_side = sel["upper"][n] if fwd else jnp.logical_not(sel["upper"][n])
        gl = jnp.where(q_side, lf2, 0.0)
    else:
        c3 = cum.reshape(L // n, n, HG_DK)
        cm = jnp.broadcast_to(c3[:, ref_row:ref_row + 1, :], c3.shape).reshape(L, HG_DK)
        gl = -jnp.abs(cum - cm)
    e = jnp.exp2(gl)
    return (q * e).astype(BF16), (k * e).astype(BF16)


def _hg_chunks(items, sel):
    L = HG_CHUNK
    cats = []
    for q, lf2, v, tri3, fwd in items:
        hi = lf2.astype(BF16)
        r1 = lf2 - hi.astype(F32)
        mid = r1.astype(BF16)
        lo = (r1 - mid.astype(F32)).astype(BF16)
        cats.append(jnp.concatenate([hi, mid, lo], axis=0))
    cums = [_dot(it[3], cat) for it, cat in zip(items, cats)]
    ks = [1.0 - jnp.exp2(it[1]) for it in items]
    accs = [jnp.where(sel["eye"], jnp.sum(it[0] * k, axis=-1, keepdims=True), 0.0)
            for it, k in zip(items, ks)]
    for n in HG_LEVELS:
        ops = [_hg_level_operands(it[0], k, it[1], cum, n, it[4], sel)
               for it, k, cum in zip(items, ks, cums)]
        prods = [_dot_nt(qt, kt) for qt, kt in ops]
        accs = [jnp.where(sel["own"][it[4]][n], p, a) for it, p, a in zip(items, prods, accs)]
    o_intras = [_dot(a.astype(BF16), it[2]) for a, it in zip(accs, items)]
    edges = [cum[L - 1:L] if it[4] else cum[0:1] for it, cum in zip(items, cums)]
    khs = [(k * jnp.exp2(e - cum)).astype(BF16) for k, e, cum in zip(ks, edges, cums)]
    us = [_dot_tn(it[2], kh) for it, kh in zip(items, khs)]
    qbs = [(it[0] * jnp.exp2(cum)).astype(BF16) for it, cum in zip(items, cums)]
    return [(o, qb, u, jnp.exp2(e)) for o, qb, u, e in zip(o_intras, qbs, us, edges)]


def _hg_scan_kernel(q_ref, lf_ref, lb_ref, v_ref, z_ref, g_ref, o_ref,
                    of_ref, ob_ref, sf_ref, sb_ref, *, nchunks, fin_rows):
    L = HG_CHUNK
    t_i = lax.broadcasted_iota(jnp.int32, (L, L), 0)
    s_i = lax.broadcasted_iota(jnp.int32, (L, L), 1)
    x_i = t_i ^ s_i
    row = lax.broadcasted_iota(jnp.int32, (L, HG_DK), 0)
    sel = {"eye": t_i == s_i,
           "own": {True: {n: (x_i < n) & (x_i >= n // 2) & (t_i > s_i) for n in HG_LEVELS},
                   False: {n: (x_i < n) & (x_i >= n // 2) & (t_i < s_i) for n in HG_LEVELS}},
           "upper": {n: (row & (n - 1)) >= n // 2 for n in HG_LEVELS if n < 16}}
    t3 = lax.broadcasted_iota(jnp.int32, (L, 3 * L), 0)
    u3 = lax.broadcasted_iota(jnp.int32, (L, 3 * L), 1) & (L - 1)
    tri_f = (u3 <= t3).astype(BF16)
    tri_b = (u3 >= t3).astype(BF16)
    sf_ref[...] = jnp.zeros_like(sf_ref)
    sb_ref[...] = jnp.zeros_like(sb_ref)

    def body(i, carry):
        dirs = ((True, lf_ref, sf_ref, of_ref, tri_f), (False, lb_ref, sb_ref, ob_ref, tri_b))
        items, rows_of = [], []
        for fwd, gate_ref, _, _, tri in dirs:
            for jj in range(HG_UNROLL):
                c = i * HG_UNROLL + jj
                c = c if fwd else nchunks - 1 - c
                rows = pl.ds(pl.multiple_of(c * L, L), L)
                rows_of.append(rows)
                items.append((q_ref[rows, :].astype(F32), gate_ref[rows, :], v_ref[rows, :], tri, fwd))
        parts = _hg_chunks(items, sel)
        sts = [d[2][...] for d in dirs]
        for jj in range(HG_UNROLL):
            for di, (_, _, _, out_ref, _) in enumerate(dirs):
                o_intra, qb, u, dec = parts[di * HG_UNROLL + jj]
                out_ref[rows_of[di * HG_UNROLL + jj], :] = o_intra + _dot_nt(qb, sts[di].astype(BF16))
                sts[di] = sts[di] * dec + u
        for d, st in zip(dirs, sts):
            d[2][...] = st
        return carry

    lax.fori_loop(0, nchunks // HG_UNROLL, body, 0)

    def fin(i, carry):
        rows = pl.ds(pl.multiple_of(i * fin_rows, fin_rows), fin_rows)
        o = of_ref[rows, :] + ob_ref[rows, :]
        y = _rmsnorm(o, g_ref[...])
        o_ref[rows, :] = (y * _silu(z_ref[rows, :].astype(F32))).astype(BF16)
        return carry

    lax.fori_loop(0, (nchunks * L) // fin_rows, fin, 0)


def _hg_scan(q, lf, lb, v, z, g_out):
    b, s, _ = q.shape
    head = pl.BlockSpec((None, s, HG_DK), lambda bb, h: (bb, 0, h))
    fin_rows = min(256, s)
    return pl.pallas_call(
        functools.partial(_hg_scan_kernel, nchunks=s // HG_CHUNK, fin_rows=fin_rows),
        grid=(b, HG_HEADS),
        in_specs=[head] * 5 + [pl.BlockSpec((1, HG_DK), lambda bb, h: (0, h))],
        out_specs=head,
        out_shape=jax.ShapeDtypeStruct((b, s, BRANCH), BF16),
        scratch_shapes=[pltpu.VMEM((s, HG_DK), F32), pltpu.VMEM((s, HG_DK), F32),
                        pltpu.VMEM((HG_DK, HG_DK), F32), pltpu.VMEM((HG_DK, HG_DK), F32)],
        compiler_params=_cparams(("parallel", "parallel")),
        name="hg_scan",
    )(q, lf, lb, v, z, g_out)


def _rot_half_cols(w):
    half = MLA_ROPE // 2
    return jnp.concatenate([-w[..., half:], w[..., :half]], axis=-1)


def _prep(s, norm_g, fn_w_in, fn_w_mix, fn_w_out, na_w_in, na_rpb, na_w_out,
          mla_w_in, mla_g_q, mla_w_uq, mla_g_kv, mla_w_ukv, mla_w_out,
          hg_w_in, hg_lb_raw, hg_g_out, hg_w_out, ple_w, ple_gate_w, final_g, tk):
    bf = lambda t: t.astype(BF16)
    row = lambda t: t.reshape(1, -1).astype(F32)
    w = {}
    w["norm_g"] = [row(norm_g[i]) for i in range(DEPTH)]
    w["final_g"] = row(final_g)
    w["ple_w"] = [bf(ple_w[i]) for i in range(DEPTH)]
    w["gate_w"] = [bf(ple_gate_w[i]) for i in range(DEPTH)]
    w["w_out"] = [bf(fn_w_out[0]), bf(na_w_out[0]), bf(mla_w_out[0]), bf(hg_w_out[0])]
    w["fn_w_in"] = bf(fn_w_in[0])
    w["fn_w2"] = _fn_fold(fn_w_mix[0].astype(F32), s)
    w["dft"] = _dft_mats(s, tk)
    w["na_w_in"] = bf(na_w_in[0])
    w["na_table"] = _na_bias_table(na_rpb[0].astype(F32))
    wi = mla_w_in[0]
    c0, c1, c2 = MLA_Q_LORA, MLA_Q_LORA + MLA_KV_LORA, MLA_Q_LORA + MLA_KV_LORA + MLA_ROPE
    wkpe = wi[:, c1:c2]
    wrot = _rot_half_cols(wkpe)
    w["mla_w1"] = bf(jnp.concatenate([wi[:, :c1], wkpe, wkpe, wrot, wrot, wi[:, c2:]], axis=1))
    wq = mla_w_uq[0].reshape(MLA_Q_LORA, MLA_HEADS, MLA_NOPE + MLA_ROPE)
    wq_pe = wq[..., MLA_NOPE:]
    w["mla_wuq"] = bf(jnp.concatenate([wq[..., :MLA_NOPE], wq_pe, _rot_half_cols(wq_pe)], axis=-1)
                      .reshape(MLA_Q_LORA, MLA_HEADS * MLA_QK))
    wkv = mla_w_ukv[0].reshape(MLA_KV_LORA, MLA_HEADS, MLA_NOPE + MLA_V)
    w["mla_wuk"] = bf(wkv[..., :MLA_NOPE].reshape(MLA_KV_LORA, BRANCH))
    w["mla_wuvt"] = bf(wkv[..., MLA_NOPE:].reshape(MLA_KV_LORA, BRANCH).T)
    w["mla_gq"] = row(mla_g_q[0])
    w["mla_gkv"] = row(mla_g_kv[0])
    half = MLA_ROPE // 2
    inv = ROPE_THETA ** (-jnp.arange(half, dtype=F32) / half)
    ang = jnp.arange(s, dtype=F32)[:, None] * inv[None, :]
    cos, sin = jnp.cos(ang), jnp.sin(ang)
    cos64 = jnp.concatenate([cos, cos], axis=-1)
    sin64 = jnp.concatenate([sin, sin], axis=-1)
    scale = (MLA_NOPE + MLA_ROPE) ** -0.5 * math.log2(math.e)
    w["mla_rq"] = scale * jnp.concatenate([jnp.ones((s, MLA_NOPE), F32), cos64, sin64], axis=-1)
    w["mla_rk"] = jnp.concatenate([cos64, cos64, sin64, sin64], axis=-1)
    w["hg_w_in"] = bf(hg_w_in[0])
    w["hg_lb_raw"] = hg_lb_raw.astype(F32)
    w["hg_g_out"] = row(hg_g_out[0])
    return w


def _trunk(x, p, w, tm, tk, tq, tkv):
    def finish(x, o, li, final=False):
        return _out_proj(x, o, p, li, w["w_out"][li], w["gate_w"][li], w["ple_w"][li],
                         w["final_g"] if final else None, tm)

    u, z = _fn_in(x, w["norm_g"][0], w["fn_w_in"], tm)
    o = _fn_dft(w["dft"], u, z, w["fn_w2"], tk)
    x = finish(x, o, 0)
    q, k, v, z = _na_in(x, w["norm_g"][1], w["na_w_in"], tm)
    o = _na_attn(q, k, v, z, w["na_table"])
    x = finish(x, o, 1)
    q, k, vt, z = _mla_in(x, w["norm_g"][2], w["mla_w1"], w["mla_gq"], w["mla_wuq"], w["mla_gkv"],
                          w["mla_wuk"], w["mla_wuvt"], w["mla_rq"], w["mla_rk"], tm)
    o = _mla_attn(q, k, vt, z, tq, tkv)
    x = finish(x, o, 2)
    q, lf, lb, v, z = _hg_in(x, w["norm_g"][3], w["hg_w_in"], w["hg_lb_raw"], 3, tm)
    o = _hg_scan(q, lf, lb, v, z, w["hg_g_out"])
    return finish(x, o, 3, final=True)


def kernel(x_prompt, x_sample, p_prompt, p_sample, norm_g, fn_w_in, fn_w_mix, fn_w_out, na_w_in, na_rpb, na_w_out, mla_w_in, mla_g_q, mla_w_uq, mla_g_kv, mla_w_ukv, mla_w_out, hg_w_in, hg_lb_raw, hg_g_out, hg_w_out, ple_w, ple_gate_w, final_g):
    s = x_prompt.shape[1]
    assert x_sample.shape[1] == s
    tm = min(512, s)
    tk = min(512, s)
    tq = min(1024, s)
    tkv = min(1024, s)
    w = _prep(s, norm_g, fn_w_in, fn_w_mix, fn_w_out, na_w_in, na_rpb, na_w_out,
              mla_w_in, mla_g_q, mla_w_uq, mla_g_kv, mla_w_ukv, mla_w_out,
              hg_w_in, hg_lb_raw, hg_g_out, hg_w_out, ple_w, ple_gate_w, final_g, tk)
    y_prompt = _trunk(x_prompt, p_prompt, w, tm, tk, tq, tkv)
    y_sample = _trunk(x_sample, p_sample, w, tm, tk, tq, tkv)
    return (y_prompt, y_sample)
```

```python
import functools
import math

import numpy as np
import jax
import jax.numpy as jnp
from jax import lax
from jax.experimental import pallas as pl
from jax.experimental.pallas import tpu as pltpu

F32 = jnp.float32
BF16 = jnp.bfloat16

D_MODEL = 1024
DEPTH = 4
BRANCH = 1024
PLE_DIM = 256
GRID_W = 64
EPS = 1e-6
LANE = 128
VMEM_LIMIT = 56 * 1024 * 1024

FN_GROUPS = 8
FN_GDIM = BRANCH // FN_GROUPS

NA_HEADS = 32
NA_HDIM = BRANCH // NA_HEADS
NA_KH = 8
NA_KW = 16
NA_HPG = LANE // NA_HDIM
NA_GROUPS = NA_HEADS // NA_HPG
NA_NEG = -1e30

MLA_HEADS = 8
MLA_NOPE = 128
MLA_ROPE = 64
MLA_V = BRANCH // MLA_HEADS
MLA_Q_LORA = 384
MLA_KV_LORA = 256
MLA_QK = MLA_NOPE + 2 * MLA_ROPE
MLA_QTILES = 2
ROPE_THETA = 10000.0

HG_HEADS = 8
HG_DK = BRANCH // HG_HEADS
HG_CHUNK = 64
HG_LEVELS = (64, 32, 16, 8, 4, 2)
HG_UNROLL = 4
NA_UNROLL = 8
LOG2E = math.log2(math.e)


def _cparams(sem):
    return pltpu.CompilerParams(dimension_semantics=sem, vmem_limit_bytes=VMEM_LIMIT)


def _rmsnorm(x, g):
    ms = jnp.mean(x * x, axis=-1, keepdims=True)
    return x * lax.rsqrt(ms + EPS) * g


def _sigmoid(x):
    return 1.0 / (1.0 + jnp.exp(-x))


def _silu(x):
    return x * _sigmoid(x)


def _dot(a, b):
    return jnp.dot(a, b, preferred_element_type=F32)


def _dot_nt(a, b):
    return lax.dot_general(a, b, (((1,), (1,)), ((), ())), preferred_element_type=F32)


def _dot_tn(a, b):
    return lax.dot_general(a, b, (((0,), (0,)), ((), ())), preferred_element_type=F32)


def _tok_spec(tm, width):
    return pl.BlockSpec((None, tm, width), lambda b, i: (b, i, 0))


def _full_spec(shape):
    nd = len(shape)
    return pl.BlockSpec(shape, lambda *_: (0,) * nd)


def _fn_in_kernel(x_ref, g_ref, w_ref, u_ref, z_ref):
    h = _rmsnorm(x_ref[...], g_ref[...]).astype(BF16)
    u_ref[...] = _dot(h, w_ref[:, :BRANCH]).astype(BF16)
    z_ref[...] = _dot(h, w_ref[:, BRANCH:]).astype(BF16)


def _fn_in(x, g, w, tm):
    b, s, d = x.shape
    return pl.pallas_call(
        _fn_in_kernel,
        grid=(b, s // tm),
        in_specs=[_tok_spec(tm, d), _full_spec((1, d)), _full_spec((d, 2 * BRANCH))],
        out_specs=[_tok_spec(tm, BRANCH)] * 2,
        out_shape=[jax.ShapeDtypeStruct((b, s, BRANCH), BF16)] * 2,
        compiler_params=_cparams(("parallel", "parallel")),
        name="fn_in",
    )(x, g, w)


def _fn_fold_kernel(c_ref, s_ref, w_ref, o_ref, *, scale):
    w = w_ref[...]
    hi = lax.Precision.HIGHEST
    wc = (scale * jnp.dot(c_ref[...], w, precision=hi, preferred_element_type=F32)).astype(BF16)
    ws = (scale * jnp.dot(s_ref[...], w, precision=hi, preferred_element_type=F32)).astype(BF16)
    o_ref[:FN_GDIM, :FN_GDIM] = wc
    o_ref[:FN_GDIM, FN_GDIM:] = wc
    o_ref[FN_GDIM:, :FN_GDIM] = -ws
    o_ref[FN_GDIM:, FN_GDIM:] = ws


def _fn_fold(w_mix, s):
    c = jnp.arange(FN_GDIM, dtype=jnp.int32)
    ang = ((c[:, None] * c[None, :]) % FN_GDIM).astype(F32) * (2.0 * math.pi / FN_GDIM)
    scale = 1.0 / math.sqrt(s * FN_GDIM)
    gspec = pl.BlockSpec((None, FN_GDIM, FN_GDIM), lambda g: (g, 0, 0))
    return pl.pallas_call(
        functools.partial(_fn_fold_kernel, scale=scale),
        grid=(FN_GROUPS,),
        in_specs=[_full_spec((FN_GDIM, FN_GDIM))] * 2 + [gspec],
        out_specs=pl.BlockSpec((None, 2 * FN_GDIM, 2 * FN_GDIM), lambda g: (g, 0, 0)),
        out_shape=jax.ShapeDtypeStruct((FN_GROUPS, 2 * FN_GDIM, 2 * FN_GDIM), BF16),
        name="fn_fold",
    )(jnp.cos(ang), jnp.sin(ang), w_mix)


def _dft_mats(s, tk):
    nt = s // 2 // tk
    k = jnp.arange(s // 2, dtype=jnp.int32)
    pos = jnp.arange(s, dtype=jnp.int32)
    ang = ((k[:, None] * pos[None, :]) % s).astype(F32) * (2.0 * math.pi / s)
    c = jnp.cos(ang).astype(BF16).reshape(nt, tk, s)
    sn = jnp.sin(ang).astype(BF16).reshape(nt, tk, s)
    nyq = jnp.where(pos % 2 == 0, 1.0, -1.0).astype(BF16)
    e = jnp.concatenate([nyq[None, :], jnp.zeros((7, s), BF16)], axis=0)
    r = np.arange(tk)
    rev = np.zeros((tk, tk), np.float32)
    rev[r[1:], tk - r[1:]] = 1.0
    return jnp.concatenate([c, sn], axis=1), e, jnp.asarray(rev, BF16)


def _fn_dft_kernel(m_ref, e_ref, rev_ref, u_ref, zd_ref, zm_ref, w2_ref, o_ref, r_ref, carry_ref, *, tk):
    u = u_ref[...]

    @pl.when(pl.program_id(1) == 0)
    def _():
        ev = _dot(e_ref[...], u).astype(BF16)
        for g in range(FN_GROUPS):
            sl = slice(g * FN_GDIM, (g + 1) * FN_GDIM)
            ab = jnp.concatenate([ev[:, sl], jnp.zeros_like(ev[:, sl])], axis=1)
            carry_ref[:, sl] = _dot(ab, w2_ref[g])[:, FN_GDIM:]

    r = _dot(m_ref[...], u)
    a = r[:tk].astype(BF16)
    bm = r[tk:].astype(BF16)
    carry = carry_ref[0:1, :]
    for g in range(FN_GROUPS):
        sl = slice(g * FN_GDIM, (g + 1) * FN_GDIM)
        ab = jnp.concatenate([a[:, sl], bm[:, sl]], axis=1)
        y2 = _dot(ab, w2_ref[g])
        o_ref[0, :, sl] = (y2[:, :FN_GDIM] * _silu(zd_ref[:, sl].astype(F32))).astype(BF16)
        r_ref[:, sl] = y2[:, FN_GDIM:].astype(BF16)
        carry_ref[:, sl] = y2[0:8, FN_GDIM:]
    mirror = _dot(rev_ref[...], r_ref[...])
    row0 = lax.broadcasted_iota(jnp.int32, (tk, BRANCH), 0) == 0
    mirror = jnp.where(row0, carry, mirror)
    o_ref[1] = (mirror * _silu(zm_ref[...].astype(F32))).astype(BF16)


def _fn_dft(mats, u, z, w2, tk):
    m, e, rev = mats
    b, s, _ = u.shape
    nt = s // 2 // tk
    return pl.pallas_call(
        functools.partial(_fn_dft_kernel, tk=tk),
        grid=(b, nt),
        in_specs=[pl.BlockSpec((None, 2 * tk, s), lambda bb, jj: (nt - 1 - jj, 0, 0)),
                  _full_spec((8, s)), _full_spec((tk, tk)),
                  pl.BlockSpec((None, s, BRANCH), lambda bb, jj: (bb, 0, 0)),
                  pl.BlockSpec((None, tk, BRANCH), lambda bb, jj: (bb, nt - 1 - jj, 0)),
                  pl.BlockSpec((None, tk, BRANCH), lambda bb, jj: (bb, nt + jj, 0)),
                  _full_spec((FN_GROUPS, 2 * FN_GDIM, 2 * FN_GDIM))],
        out_specs=pl.BlockSpec((None, 2, tk, BRANCH), lambda bb, jj: (bb, 0, nt - 1 - jj, 0)),
        out_shape=jax.ShapeDtypeStruct((b, 2, s // 2, BRANCH), BF16),
        scratch_shapes=[pltpu.VMEM((tk, BRANCH), BF16), pltpu.VMEM((8, BRANCH), F32)],
        compiler_params=_cparams(("parallel", "arbitrary")),
        name="fn_dft",
    )(m, e, rev, u, z, z, w2)


def _out_kernel(x_ref, o_ref, p_ref, wo_ref, gw_ref, pw_ref, *rest, final):
    y_ref = rest[-1]
    x1 = x_ref[...] + _dot(o_ref[...], wo_ref[...])
    gate = _sigmoid(_dot(x1.astype(BF16), gw_ref[...]))
    x2 = x1 + gate * _dot(p_ref[...].astype(BF16), pw_ref[...])
    if final:
        x2 = _rmsnorm(x2, rest[0][...])
    y_ref[...] = x2


def _out_proj(x, o, p, li, wo, gw, pw, final_g, tm):
    b, s, d = x.shape
    final = final_g is not None
    if o.ndim == 4:
        nt = s // 2 // tm
        assert o.shape == (b, 2, nt * tm, BRANCH)
        o_spec = pl.BlockSpec((None, None, tm, BRANCH),
                              lambda bb, i: (bb, i // nt, jnp.where(i < nt, i, 2 * nt - 1 - i), 0))
    else:
        o_spec = _tok_spec(tm, BRANCH)
    in_specs = [_tok_spec(tm, d), o_spec,
                pl.BlockSpec((None, None, tm, PLE_DIM), lambda bb, i: (li, bb, i, 0)),
                _full_spec((BRANCH, d)), _full_spec((d, d)), _full_spec((PLE_DIM, d))]
    args = [x, o, p, wo, gw, pw]
    if final:
        in_specs.append(_full_spec((1, d)))
        args.append(final_g)
    return pl.pallas_call(
        functools.partial(_out_kernel, final=final),
        grid=(b, s // tm),
        in_specs=in_specs,
        out_specs=_tok_spec(tm, d),
        out_shape=jax.ShapeDtypeStruct((b, s, d), F32),
        compiler_params=_cparams(("parallel", "parallel")),
        name="out_proj",
    )(*args)


def _na_in_kernel(x_ref, g_ref, w_ref, q_ref, k_ref, v_ref, z_ref):
    h = _rmsnorm(x_ref[...], g_ref[...]).astype(BF16)
    q_ref[...] = (_dot(h, w_ref[:, :BRANCH]) * (NA_HDIM ** -0.5 * LOG2E)).astype(BF16)
    k_ref[...] = _dot(h, w_ref[:, BRANCH:2 * BRANCH]).astype(BF16)
    v_ref[...] = _dot(h, w_ref[:, 2 * BRANCH:3 * BRANCH]).astype(BF16)
    z_ref[...] = _dot(h, w_ref[:, 3 * BRANCH:]).astype(BF16)


def _na_in(x, g, w, tm):
    b, s, d = x.shape
    tok = jax.ShapeDtypeStruct((b, s, BRANCH), BF16)
    return pl.pallas_call(
        _na_in_kernel,
        grid=(b, s // tm),
        in_specs=[_tok_spec(tm, d), _full_spec((1, d)), _full_spec((d, 4 * BRANCH))],
        out_specs=[_tok_spec(tm, BRANCH)] * 4,
        out_shape=[tok] * 4,
        compiler_params=_cparams(("parallel", "parallel")),
        name="na_in",
    )(x, g, w)


def _na_bias_table(rpb):
    d = np.arange(NA_KH)
    i = np.arange(NA_KH)
    ro = i[None, :] - d[:, None] + (NA_KH - 1)
    c = np.arange(GRID_W)
    kc = np.arange(GRID_W)
    win = np.clip(c - NA_KW // 2, 0, GRID_W - NA_KW)
    valid = (kc[:, None] >= win[None, :]) & (kc[:, None] < win[None, :] + NA_KW)
    pad = GRID_W - NA_KW
    r1 = jnp.pad(rpb[:, ro, :] * LOG2E, ((0, 0), (0, 0), (0, 0), (pad, pad)))
    cols = [r1[..., GRID_W - 1 - cc:2 * GRID_W - 1 - cc] for cc in range(GRID_W)]
    bias = jnp.stack(cols, axis=-1)
    bias = jnp.where(jnp.asarray(valid)[None, None, None], bias, NA_NEG)
    bias = bias.reshape(NA_GROUPS, NA_HPG, NA_KH, NA_KH, GRID_W, GRID_W)
    bias = jnp.transpose(bias, (0, 2, 3, 4, 1, 5))
    return bias.reshape(NA_GROUPS, NA_KH, NA_KH * GRID_W, NA_HPG * GRID_W).astype(F32)


def _na_attn_kernel(q_ref, k_ref, v_ref, z_ref, t_ref, o_ref, ot_ref, *, rows):
    lane = lax.broadcasted_iota(jnp.int32, (GRID_W, LANE), 1)
    head_masks = [(lane >= hh * NA_HDIM) & (lane < (hh + 1) * NA_HDIM) for hh in range(NA_HPG)]
    row_head = lax.broadcasted_iota(jnp.int32, (LANE, LANE), 0) // NA_HDIM
    low_heads = row_head < NA_HPG // 2
    even_heads = (row_head & 1) == 0
    nkeys = NA_KH * GRID_W

    def logits(r):
        rs = jnp.clip(r - NA_KH // 2, 0, rows - NA_KH)
        q_r = q_ref[pl.ds(pl.multiple_of(r * GRID_W, GRID_W), GRID_W), :]
        zero = jnp.zeros_like(q_r)
        qs = jnp.concatenate([jnp.where(m, q_r, zero) for m in head_masks], axis=0)
        kwin = k_ref[pl.ds(pl.multiple_of(rs * GRID_W, GRID_W), nkeys), :]
        return _dot_nt(kwin, qs) + t_ref[r - rs]

    def probs(sc):
        p = jnp.exp2(sc - jnp.max(sc, axis=0, keepdims=True))
        return p.astype(BF16), jnp.sum(p, axis=0, keepdims=True)

    def values(r, p, l):
        rs = jnp.clip(r - NA_KH // 2, 0, rows - NA_KH)
        vwin = v_ref[pl.ds(pl.multiple_of(rs * GRID_W, GRID_W), nkeys), :]
        return _dot_tn(vwin, p) * (1.0 / l)

    def emit(r, ot):
        w = jnp.where(low_heads, ot[:, :LANE], ot[:, LANE:])
        w = jnp.where(even_heads, w, pltpu.roll(w, GRID_W, axis=1))
        o = w.T[:GRID_W]
        q0 = pl.multiple_of(r * GRID_W, GRID_W)
        zz = z_ref[pl.ds(q0, GRID_W), :].astype(F32)
        o_ref[pl.ds(q0, GRID_W), :] = (o * _silu(zz)).astype(BF16)

    def step(i, emit_previous):
        rws = [i * NA_UNROLL + u for u in range(NA_UNROLL)]
        scs = [logits(r) for r in rws]
        if emit_previous:
            for u in range(NA_UNROLL):
                emit((i - 1) * NA_UNROLL + u, ot_ref[u])
        pls = [probs(sc) for sc in scs]
        for u, (r, (p, l)) in enumerate(zip(rws, pls)):
            ot_ref[u] = values(r, p, l)

    def body(i, carry):
        step(i, True)
        return carry

    nsteps = rows // NA_UNROLL
    step(0, False)
    lax.fori_loop(1, nsteps, body, 0)
    for u in range(NA_UNROLL):
        emit((nsteps - 1) * NA_UNROLL + u, ot_ref[u])


def _na_attn(q, k, v, z, table):
    b, s, _ = q.shape
    rows = s // GRID_W
    assert rows >= NA_KH
    grp = pl.BlockSpec((None, s, LANE), lambda g, bb: (bb, 0, g))
    return pl.pallas_call(
        functools.partial(_na_attn_kernel, rows=rows),
        grid=(NA_GROUPS, b),
        in_specs=[grp, grp, grp, grp,
                  pl.BlockSpec((None, NA_KH, NA_KH * GRID_W, NA_HPG * GRID_W),
                               lambda g, bb: (g, 0, 0, 0))],
        out_specs=grp,
        out_shape=jax.ShapeDtypeStruct((b, s, BRANCH), BF16),
        scratch_shapes=[pltpu.VMEM((NA_UNROLL, LANE, NA_HPG * GRID_W), F32)],
        compiler_params=_cparams(("parallel", "parallel")),
        name="na_attn",
    )(q, k, v, z, table)


def _mla_in_kernel(x_ref, g_ref, w1_ref, gq_ref, wuq_ref, gkv_ref, wuk_ref, wuvt_ref,
                   rq_ref, rk_ref, q_ref, k_ref, vt_ref, z_ref):
    h = _rmsnorm(x_ref[...], g_ref[...]).astype(BF16)
    c0, c1, c2, c3 = MLA_Q_LORA, MLA_Q_LORA + MLA_KV_LORA, MLA_Q_LORA + MLA_KV_LORA + LANE, \
        MLA_Q_LORA + MLA_KV_LORA + 2 * LANE
    cq = _dot(h, w1_ref[:, :c0])
    ckv = _dot(h, w1_ref[:, c0:c1])
    kp = _dot(h, w1_ref[:, c1:c2])
    kr = _dot(h, w1_ref[:, c2:c3])
    z_ref[...] = _dot(h, w1_ref[:, c3:]).astype(BF16)
    cqn = _rmsnorm(cq, gq_ref[...]).astype(BF16)
    ckvn = _rmsnorm(ckv, gkv_ref[...]).astype(BF16)
    rq = rq_ref[...]
    for hh in range(MLA_HEADS):
        sl = slice(hh * MLA_QK, (hh + 1) * MLA_QK)
        q_ref[:, sl] = (_dot(cqn, wuq_ref[:, sl]) * rq).astype(BF16)
    krr = (kp * rk_ref[:, :LANE] + kr * rk_ref[:, LANE:]).astype(BF16)
    kn = _dot(ckvn, wuk_ref[...]).astype(BF16)
    for hh in range(MLA_HEADS):
        k_ref[:, hh * MLA_QK:hh * MLA_QK + MLA_NOPE] = kn[:, hh * MLA_NOPE:(hh + 1) * MLA_NOPE]
        k_ref[:, hh * MLA_QK + MLA_NOPE:(hh + 1) * MLA_QK] = krr
    vt_ref[...] = _dot_nt(wuvt_ref[...], ckvn).astype(BF16)


def _mla_in(x, g, w1, gq, wuq, gkv, wuk, wuvt, rq, rk, tm):
    b, s, d = x.shape
    n1 = w1.shape[1]
    qk = MLA_HEADS * MLA_QK
    rope_spec = pl.BlockSpec((tm, 2 * LANE), lambda bb, i: (i, 0))
    return pl.pallas_call(
        _mla_in_kernel,
        grid=(b, s // tm),
        in_specs=[_tok_spec(tm, d), _full_spec((1, d)), _full_spec((d, n1)),
                  _full_spec((1, MLA_Q_LORA)), _full_spec((MLA_Q_LORA, qk)),
                  _full_spec((1, MLA_KV_LORA)), _full_spec((MLA_KV_LORA, BRANCH)),
                  _full_spec((BRANCH, MLA_KV_LORA)), rope_spec, rope_spec],
        out_specs=[_tok_spec(tm, qk), _tok_spec(tm, qk),
                   pl.BlockSpec((None, BRANCH, tm), lambda bb, i: (bb, 0, i)), _tok_spec(tm, BRANCH)],
        out_shape=[jax.ShapeDtypeStruct((b, s, qk), BF16), jax.ShapeDtypeStruct((b, s, qk), BF16),
                   jax.ShapeDtypeStruct((b, BRANCH, s), BF16), jax.ShapeDtypeStruct((b, s, BRANCH), BF16)],
        compiler_params=_cparams(("parallel", "parallel")),
        name="mla_in",
    )(x, g, w1, gq, wuq, gkv, wuk, wuvt, rq, rk)


def _mla_attn_kernel(q_ref, k_ref, vt_ref, z_ref, o_ref, *, tq, tk, nk, nq):
    items = [(t, j) for t in range(nq) for j in range(nk)]
    rows = lambda t: slice(t * tq, (t + 1) * tq)
    logits = lambda t, j: _dot_nt(k_ref[j * tk:(j + 1) * tk, :], q_ref[rows(t), :])
    st_next = logits(*items[0])
    for idx, (t, j) in enumerate(items):
        st = st_next
        if idx + 1 < len(items):
            st_next = logits(*items[idx + 1])
        if j == 0:
            m = jnp.full((1, tq), -jnp.inf, F32)
            l = jnp.zeros((1, tq), F32)
            acc = jnp.zeros((MLA_V, tq), F32)
        m_new = jnp.maximum(m, jnp.max(st, axis=0, keepdims=True))
        alpha = jnp.exp2(m - m_new)
        p = jnp.exp2(st - m_new)
        l = alpha * l + jnp.sum(p, axis=0, keepdims=True)
        acc = alpha * acc + _dot(vt_ref[:, j * tk:(j + 1) * tk], p.astype(BF16))
        m = m_new
        if j == nk - 1:
            o = (acc * (1.0 / l)).T
            o_ref[rows(t), :] = (o * _silu(z_ref[rows(t), :].astype(F32))).astype(BF16)


def _mla_attn(q, k, vt, z, tq, tk):
    b, _, s = vt.shape
    nq = min(MLA_QTILES, s // tq)
    tb = nq * tq
    return pl.pallas_call(
        functools.partial(_mla_attn_kernel, tq=tq, tk=tk, nk=s // tk, nq=nq),
        grid=(b, MLA_HEADS, s // tb),
        in_specs=[pl.BlockSpec((None, tb, MLA_QK), lambda bb, h, i: (bb, i, h)),
                  pl.BlockSpec((None, s, MLA_QK), lambda bb, h, i: (bb, 0, h)),
                  pl.BlockSpec((None, MLA_V, s), lambda bb, h, i: (bb, h, 0)),
                  pl.BlockSpec((None, tb, MLA_V), lambda bb, h, i: (bb, i, h))],
        out_specs=pl.BlockSpec((None, tb, MLA_V), lambda bb, h, i: (bb, i, h)),
        out_shape=jax.ShapeDtypeStruct((b, s, BRANCH), BF16),
        compiler_params=_cparams(("parallel", "parallel", "arbitrary")),
        name="mla_attn",
    )(q, k, vt, z)


def _hg_in_kernel(x_ref, g_ref, w_ref, lbr_ref, q_ref, lf_ref, lb_ref, i_ref, z_ref, *, li):
    h = _rmsnorm(x_ref[...], g_ref[...]).astype(BF16)
    raw = lbr_ref[...]
    e = jnp.exp(raw - jnp.max(raw, axis=0, keepdims=True))
    sm = e / jnp.sum(e, axis=0, keepdims=True)
    lower = jnp.zeros_like(sm[0])
    for dd in range(1, li + 1):
        lower = lower + sm[dd]
    q_ref[...] = _silu(_dot(h, w_ref[:, :BRANCH])).astype(BF16)
    for idx, ref in ((0, lf_ref), (1, lb_ref)):
        lb = lower[idx:idx + 1]
        raw_f = _dot(h, w_ref[:, (1 + idx) * BRANCH:(2 + idx) * BRANCH])
        ref[...] = jnp.log(lb + (1.0 - lb) * _sigmoid(raw_f)) * LOG2E
    i_ref[...] = _dot(h, w_ref[:, 3 * BRANCH:4 * BRANCH]).astype(BF16)
    z_ref[...] = _dot(h, w_ref[:, 4 * BRANCH:]).astype(BF16)


def _hg_in(x, g, w, lb_raw, li, tm):
    b, s, d = x.shape
    bf = jax.ShapeDtypeStruct((b, s, BRANCH), BF16)
    f32 = jax.ShapeDtypeStruct((b, s, BRANCH), F32)
    return pl.pallas_call(
        functools.partial(_hg_in_kernel, li=li),
        grid=(b, s // tm),
        in_specs=[_tok_spec(tm, d), _full_spec((1, d)), _full_spec((d, 5 * BRANCH)),
                  _full_spec((DEPTH, 2, BRANCH))],
        out_specs=[_tok_spec(tm, BRANCH)] * 5,
        out_shape=[bf, f32, f32, bf, bf],
        compiler_params=_cparams(("parallel", "parallel")),
        name="hg_in",
    )(x, g, w, lb_raw)


def _hg_level_operands(q, k, lf2, cum, n, fwd, sel):
    L = HG_CHUNK
    half = n // 2
    ref_row = half - 1 if fwd else half
    if n >= 16:
        zero = jnp.zeros((half, HG_DK), BF16)
        qt_parts, kt_parts = [], []
        for blk in range(L // n):
            cm = cum[blk * n + ref_row:blk * n + ref_row + 1]
            for is_upper in (False, True):
                r0 = blk * n + (half if is_upper else 0)
                seg = slice(r0, r0 + half)
                e = jnp.exp2(-jnp.abs(cum[seg] - cm))
                if is_upper == fwd:
                    qt_parts.append((q[seg] * e).astype(BF16))
                    kt_parts.append(zero)
                else:
                    qt_parts.append(zero)
                    kt_parts.append((k[seg] * e).astype(BF16))
        return jnp.concatenate(qt_parts, axis=0), jnp.concatenate(kt_parts, axis=0)
    if n == 2:
        q_side = sel["upper"][n] if fwd else jnp.logical_not(sel["upper"][n])
        gl = jnp.where(q_side, lf2, 0.0)
    else:
        c3 = cum.reshape(L // n, n, HG_DK)
        cm = jnp.broadcast_to(c3[:, ref_row:ref_row + 1, :], c3.shape).reshape(L, HG_DK)
        gl = -jnp.abs(cum - cm)
    e = jnp.exp2(gl)
    return (q * e).astype(BF16), (k * e).astype(BF16)


def _hg_chunks(items, sel):
    L = HG_CHUNK
    cats = []
    for q, lf2, v, tri3, fwd in items:
        hi = lf2.astype(BF16)
        r1 = lf2 - hi.astype(F32)
        mid = r1.astype(BF16)
        lo = (r1 - mid.astype(F32)).astype(BF16)
        cats.append(jnp.concatenate([hi, mid, lo], axis=0))
    cums = [_dot(it[3], cat) for it, cat in zip(items, cats)]
    ks = [1.0 - jnp.exp2(it[1]) for it in items]
    accs = [jnp.where(sel["eye"], jnp.sum(it[0] * k, axis=-1, keepdims=True), 0.0)
            for it, k in zip(items, ks)]
    for n in HG_LEVELS:
        ops = [_hg_level_operands(it[0], k, it[1], cum, n, it[4], sel)
               for it, k, cum in zip(items, ks, cums)]
        prods = [_dot_nt(qt, kt) for qt, kt in ops]
        accs = [jnp.where(sel["own"][it[4]][n], p, a) for it, p, a in zip(items, prods, accs)]
    o_intras = [_dot(a.astype(BF16), it[2]) for a, it in zip(accs, items)]
    edges = [cum[L - 1:L] if it[4] else cum[0:1] for it, cum in zip(items, cums)]
    khs = [(k * jnp.exp2(e - cum)).astype(BF16) for k, e, cum in zip(ks, edges, cums)]
    us = [_dot_tn(it[2], kh) for it, kh in zip(items, khs)]
    qbs = [(it[0] * jnp.exp2(cum)).astype(BF16) for it, cum in zip(items, cums)]
    return [(o, qb, u, jnp.exp2(e)) for o, qb, u, e in zip(o_intras, qbs, us, edges)]


def _hg_scan_kernel(q_ref, lf_ref, lb_ref, v_ref, z_ref, g_ref, o_ref,
                    of_ref, ob_ref, sf_ref, sb_ref, *, nchunks, fin_rows):
    L = HG_CHUNK
    t_i = lax.broadcasted_iota(jnp.int32, (L, L), 0)
    s_i = lax.broadcasted_iota(jnp.int32, (L, L), 1)
    x_i = t_i ^ s_i
    row = lax.broadcasted_iota(jnp.int32, (L, HG_DK), 0)
    sel = {"eye": t_i == s_i,
           "own": {True: {n: (x_i < n) & (x_i >= n // 2) & (t_i > s_i) for n in HG_LEVELS},
                   False: {n: (x_i < n) & (x_i >= n // 2) & (t_i < s_i) for n in HG_LEVELS}},
           "upper": {n: (row & (n - 1)) >= n // 2 for n in HG_LEVELS if n < 16}}
    t3 = lax.broadcasted_iota(jnp.int32, (L, 3 * L), 0)
    u3 = lax.broadcasted_iota(jnp.int32, (L, 3 * L), 1) & (L - 1)
    tri_f = (u3 <= t3).astype(BF16)
    tri_b = (u3 >= t3).astype(BF16)
    sf_ref[...] = jnp.zeros_like(sf_ref)
    sb_ref[...] = jnp.zeros_like(sb_ref)

    def body(i, carry):
        dirs = ((True, lf_ref, sf_ref, of_ref, tri_f), (False, lb_ref, sb_ref, ob_ref, tri_b))
        items, rows_of = [], []
        for fwd, gate_ref, _, _, tri in dirs:
            for jj in range(HG_UNROLL):
                c = i * HG_UNROLL + jj
                c = c if fwd else nchunks - 1 - c
                rows = pl.ds(pl.multiple_of(c * L, L), L)
                rows_of.append(rows)
                items.append((q_ref[rows, :].astype(F32), gate_ref[rows, :], v_ref[rows, :], tri, fwd))
        parts = _hg_chunks(items, sel)
        sts = [d[2][...] for d in dirs]
        for jj in range(HG_UNROLL):
            for di, (_, _, _, out_ref, _) in enumerate(dirs):
                o_intra, qb, u, dec = parts[di * HG_UNROLL + jj]
                out_ref[rows_of[di * HG_UNROLL + jj], :] = o_intra + _dot_nt(qb, sts[di].astype(BF16))
                sts[di] = sts[di] * dec + u
        for d, st in zip(dirs, sts):
            d[2][...] = st
        return carry

    lax.fori_loop(0, nchunks // HG_UNROLL, body, 0)

    def fin(i, carry):
        rows = pl.ds(pl.multiple_of(i * fin_rows, fin_rows), fin_rows)
        o = of_ref[rows, :] + ob_ref[rows, :]
        y = _rmsnorm(o, g_ref[...])
        o_ref[rows, :] = (y * _silu(z_ref[rows, :].astype(F32))).astype(BF16)
        return carry

    lax.fori_loop(0, (nchunks * L) // fin_rows, fin, 0)


def _hg_scan(q, lf, lb, v, z, g_out):
    b, s, _ = q.shape
    head = pl.BlockSpec((None, s, HG_DK), lambda bb, h: (bb, 0, h))
    fin_rows = min(256, s)
    return pl.pallas_call(
        functools.partial(_hg_scan_kernel, nchunks=s // HG_CHUNK, fin_rows=fin_rows),
        grid=(b, HG_HEADS),
        in_specs=[head] * 5 + [pl.BlockSpec((1, HG_DK), lambda bb, h: (0, h))],
        out_specs=head,
        out_shape=jax.ShapeDtypeStruct((b, s, BRANCH), BF16),
        scratch_shapes=[pltpu.VMEM((s, HG_DK), F32), pltpu.VMEM((s, HG_DK), F32),
                        pltpu.VMEM((HG_DK, HG_DK), F32), pltpu.VMEM((HG_DK, HG_DK), F32)],
        compiler_params=_cparams(("parallel", "parallel")),
        name="hg_scan",
    )(q, lf, lb, v, z, g_out)


def _rot_half_cols(w):
    half = MLA_ROPE // 2
    return jnp.concatenate([-w[..., half:], w[..., :half]], axis=-1)


def _prep(s, norm_g, fn_w_in, fn_w_mix, fn_w_out, na_w_in, na_rpb, na_w_out,
          mla_w_in, mla_g_q, mla_w_uq, mla_g_kv, mla_w_ukv, mla_w_out,
          hg_w_in, hg_lb_raw, hg_g_out, hg_w_out, ple_w, ple_gate_w, final_g, tk):
    bf = lambda t: t.astype(BF16)
    row = lambda t: t.reshape(1, -1).astype(F32)
    w = {}
    w["norm_g"] = [row(norm_g[i]) for i in range(DEPTH)]
    w["final_g"] = row(final_g)
    w["ple_w"] = [bf(ple_w[i]) for i in range(DEPTH)]
    w["gate_w"] = [bf(ple_gate_w[i]) for i in range(DEPTH)]
    w["w_out"] = [bf(fn_w_out[0]), bf(na_w_out[0]), bf(mla_w_out[0]), bf(hg_w_out[0])]
    w["fn_w_in"] = bf(fn_w_in[0])
    w["fn_w2"] = _fn_fold(fn_w_mix[0].astype(F32), s)
    w["dft"] = _dft_mats(s, tk)
    w["na_w_in"] = bf(na_w_in[0])
    w["na_table"] = _na_bias_table(na_rpb[0].astype(F32))
    wi = mla_w_in[0]
    c0, c1, c2 = MLA_Q_LORA, MLA_Q_LORA + MLA_KV_LORA, MLA_Q_LORA + MLA_KV_LORA + MLA_ROPE
    wkpe = wi[:, c1:c2]
    wrot = _rot_half_cols(wkpe)
    w["mla_w1"] = bf(jnp.concatenate([wi[:, :c1], wkpe, wkpe, wrot, wrot, wi[:, c2:]], axis=1))
    wq = mla_w_uq[0].reshape(MLA_Q_LORA, MLA_HEADS, MLA_NOPE + MLA_ROPE)
    wq_pe = wq[..., MLA_NOPE:]
    w["mla_wuq"] = bf(jnp.concatenate([wq[..., :MLA_NOPE], wq_pe, _rot_half_cols(wq_pe)], axis=-1)
                      .reshape(MLA_Q_LORA, MLA_HEADS * MLA_QK))
    wkv = mla_w_ukv[0].reshape(MLA_KV_LORA, MLA_HEADS, MLA_NOPE + MLA_V)
    w["mla_wuk"] = bf(wkv[..., :MLA_NOPE].reshape(MLA_KV_LORA, BRANCH))
    w["mla_wuvt"] = bf(wkv[..., MLA_NOPE:].reshape(MLA_KV_LORA, BRANCH).T)
    w["mla_gq"] = row(mla_g_q[0])
    w["mla_gkv"] = row(mla_g_kv[0])
    half = MLA_ROPE // 2
    inv = ROPE_THETA ** (-jnp.arange(half, dtype=F32) / half)
    ang = jnp.arange(s, dtype=F32)[:, None] * inv[None, :]
    cos, sin = jnp.cos(ang), jnp.sin(ang)
    cos64 = jnp.concatenate([cos, cos], axis=-1)
    sin64 = jnp.concatenate([sin, sin], axis=-1)
    scale = (MLA_NOPE + MLA_ROPE) ** -0.5 * math.log2(math.e)
    w["mla_rq"] = scale * jnp.concatenate([jnp.ones((s, MLA_NOPE), F32), cos64, sin64], axis=-1)
    w["mla_rk"] = jnp.concatenate([cos64, cos64, sin64, sin64], axis=-1)
    w["hg_w_in"] = bf(hg_w_in[0])
    w["hg_lb_raw"] = hg_lb_raw.astype(F32)
    w["hg_g_out"] = row(hg_g_out[0])
    return w


def _trunk(x, p, w, tm, tk, tq, tkv):
    tm_big = min(2 * tm, x.shape[1])

    def finish(x, o, li, final=False):
        return _out_proj(x, o, p, li, w["w_out"][li], w["gate_w"][li], w["ple_w"][li],
                         w["final_g"] if final else None, tm if o.ndim == 4 else tm_big)

    u, z = _fn_in(x, w["norm_g"][0], w["fn_w_in"], tm_big)
    o = _fn_dft(w["dft"], u, z, w["fn_w2"], tk)
    x = finish(x, o, 0)
    q, k, v, z = _na_in(x, w["norm_g"][1], w["na_w_in"], tm_big)
    o = _na_attn(q, k, v, z, w["na_table"])
    x = finish(x, o, 1)
    q, k, vt, z = _mla_in(x, w["norm_g"][2], w["mla_w1"], w["mla_gq"], w["mla_wuq"], w["mla_gkv"],
                          w["mla_wuk"], w["mla_wuvt"], w["mla_rq"], w["mla_rk"], tm)
    o = _mla_attn(q, k, vt, z, tq, tkv)
    x = finish(x, o, 2)
    q, lf, lb, v, z = _hg_in(x, w["norm_g"][3], w["hg_w_in"], w["hg_lb_raw"], 3, tm)
    o = _hg_scan(q, lf, lb, v, z, w["hg_g_out"])
    return finish(x, o, 3, final=True)


def kernel(x_prompt, x_sample, p_prompt, p_sample, norm_g, fn_w_in, fn_w_mix, fn_w_out, na_w_in, na_rpb, na_w_out, mla_w_in, mla_g_q, mla_w_uq, mla_g_kv, mla_w_ukv, mla_w_out, hg_w_in, hg_lb_raw, hg_g_out, hg_w_out, ple_w, ple_gate_w, final_g):
    s = x_prompt.shape[1]
    assert x_sample.shape[1] == s
    tm = min(512, s)
    tk = min(512, s)
    tq = min(1024, s)
    tkv = min(1024, s)
    w = _prep(s, norm_g, fn_w_in, fn_w_mix, fn_w_out, na_w_in, na_rpb, na_w_out,
              mla_w_in, mla_g_q, mla_w_uq, mla_g_kv, mla_w_ukv, mla_w_out,
              hg_w_in, hg_lb_raw, hg_g_out, hg_w_out, ple_w, ple_gate_w, final_g, tk)
    y_prompt = _trunk(x_prompt, p_prompt, w, tm, tk, tq, tkv)
    y_sample = _trunk(x_sample, p_sample, w, tm, tk, tq, tkv)
    return (y_prompt, y_sample)
```

```python
import functools
import math

import numpy as np
import jax
import jax.numpy as jnp
from jax import lax
from jax.experimental import pallas as pl
from jax.experimental.pallas import tpu as pltpu

F32 = jnp.float32
BF16 = jnp.bfloat16

D_MODEL = 1024
DEPTH = 4
BRANCH = 1024
PLE_DIM = 256
GRID_W = 64
EPS = 1e-6
LANE = 128
VMEM_LIMIT = 56 * 1024 * 1024

FN_GROUPS = 8
FN_GDIM = BRANCH // FN_GROUPS

NA_HEADS = 32
NA_HDIM = BRANCH // NA_HEADS
NA_KH = 8
NA_KW = 16
NA_HPG = LANE // NA_HDIM
NA_GROUPS = NA_HEADS // NA_HPG
NA_NEG = -1e30

MLA_HEADS = 8
MLA_NOPE = 128
MLA_ROPE = 64
MLA_V = BRANCH // MLA_HEADS
MLA_Q_LORA = 384
MLA_KV_LORA = 256
MLA_QK = MLA_NOPE + 2 * MLA_ROPE
MLA_QTILES = 4
ROPE_THETA = 10000.0

HG_HEADS = 8
HG_DK = BRANCH // HG_HEADS
HG_CHUNK = 64
HG_LEVELS = (64, 32, 16, 8, 4, 2)
HG_UNROLL = 8
NA_UNROLL = 8
LOG2E = math.log2(math.e)


def _cparams(sem):
    return pltpu.CompilerParams(dimension_semantics=sem, vmem_limit_bytes=VMEM_LIMIT)


def _rmsnorm(x, g):
    ms = jnp.mean(x * x, axis=-1, keepdims=True)
    return x * lax.rsqrt(ms + EPS) * g


def _sigmoid(x):
    return 1.0 / (1.0 + jnp.exp(-x))


def _silu(x):
    return x * _sigmoid(x)


def _dot(a, b):
    return jnp.dot(a, b, preferred_element_type=F32)


def _dot_nt(a, b):
    return lax.dot_general(a, b, (((1,), (1,)), ((), ())), preferred_element_type=F32)


def _dot_tn(a, b):
    return lax.dot_general(a, b, (((0,), (0,)), ((), ())), preferred_element_type=F32)


def _tok_spec(tm, width):
    return pl.BlockSpec((None, tm, width), lambda b, i: (b, i, 0))


def _full_spec(shape):
    nd = len(shape)
    return pl.BlockSpec(shape, lambda *_: (0,) * nd)


def _fn_in_kernel(x_ref, g_ref, w_ref, u_ref, z_ref):
    h = _rmsnorm(x_ref[...], g_ref[...]).astype(BF16)
    u_ref[...] = _dot(h, w_ref[:, :BRANCH]).astype(BF16)
    z_ref[...] = _dot(h, w_ref[:, BRANCH:]).astype(BF16)


def _fn_in(x, g, w, tm):
    b, s, d = x.shape
    return pl.pallas_call(
        _fn_in_kernel,
        grid=(b, s // tm),
        in_specs=[_tok_spec(tm, d), _full_spec((1, d)), _full_spec((d, 2 * BRANCH))],
        out_specs=[_tok_spec(tm, BRANCH)] * 2,
        out_shape=[jax.ShapeDtypeStruct((b, s, BRANCH), BF16)] * 2,
        compiler_params=_cparams(("parallel", "parallel")),
        name="fn_in",
    )(x, g, w)


def _fn_fold_kernel(c_ref, s_ref, w_ref, o_ref, *, scale):
    w = w_ref[...]
    hi = lax.Precision.HIGHEST
    wc = (scale * jnp.dot(c_ref[...], w, precision=hi, preferred_element_type=F32)).astype(BF16)
    ws = (scale * jnp.dot(s_ref[...], w, precision=hi, preferred_element_type=F32)).astype(BF16)
    o_ref[:FN_GDIM, :FN_GDIM] = wc
    o_ref[:FN_GDIM, FN_GDIM:] = wc
    o_ref[FN_GDIM:, :FN_GDIM] = -ws
    o_ref[FN_GDIM:, FN_GDIM:] = ws


def _fn_fold(w_mix, s):
    c = jnp.arange(FN_GDIM, dtype=jnp.int32)
    ang = ((c[:, None] * c[None, :]) % FN_GDIM).astype(F32) * (2.0 * math.pi / FN_GDIM)
    scale = 1.0 / math.sqrt(s * FN_GDIM)
    gspec = pl.BlockSpec((None, FN_GDIM, FN_GDIM), lambda g: (g, 0, 0))
    return pl.pallas_call(
        functools.partial(_fn_fold_kernel, scale=scale),
        grid=(FN_GROUPS,),
        in_specs=[_full_spec((FN_GDIM, FN_GDIM))] * 2 + [gspec],
        out_specs=pl.BlockSpec((None, 2 * FN_GDIM, 2 * FN_GDIM), lambda g: (g, 0, 0)),
        out_shape=jax.ShapeDtypeStruct((FN_GROUPS, 2 * FN_GDIM, 2 * FN_GDIM), BF16),
        name="fn_fold",
    )(jnp.cos(ang), jnp.sin(ang), w_mix)


def _dft_mats(s, tk):
    nt = s // 2 // tk
    k = jnp.arange(s // 2, dtype=jnp.int32)
    pos = jnp.arange(s, dtype=jnp.int32)
    ang = ((k[:, None] * pos[None, :]) % s).astype(F32) * (2.0 * math.pi / s)
    c = jnp.cos(ang).astype(BF16).reshape(nt, tk, s)
    sn = jnp.sin(ang).astype(BF16).reshape(nt, tk, s)
    nyq = jnp.where(pos % 2 == 0, 1.0, -1.0).astype(BF16)
    e = jnp.concatenate([nyq[None, :], jnp.zeros((7, s), BF16)], axis=0)
    r = np.arange(tk)
    rev = np.zeros((tk, tk), np.float32)
    rev[r[1:], tk - r[1:]] = 1.0
    return jnp.concatenate([c, sn], axis=1), e, jnp.asarray(rev, BF16)


def _fn_dft_kernel(m_ref, e_ref, rev_ref, u_ref, zd_ref, zm_ref, w2_ref, o_ref, r_ref, carry_ref, *, tk):
    u = u_ref[...]

    @pl.when(pl.program_id(1) == 0)
    def _():
        ev = _dot(e_ref[...], u).astype(BF16)
        for g in range(FN_GROUPS):
            sl = slice(g * FN_GDIM, (g + 1) * FN_GDIM)
            ab = jnp.concatenate([ev[:, sl], jnp.zeros_like(ev[:, sl])], axis=1)
            carry_ref[:, sl] = _dot(ab, w2_ref[g])[:, FN_GDIM:]

    r = _dot(m_ref[...], u)
    a = r[:tk].astype(BF16)
    bm = r[tk:].astype(BF16)
    carry = carry_ref[0:1, :]
    for g in range(FN_GROUPS):
        sl = slice(g * FN_GDIM, (g + 1) * FN_GDIM)
        ab = jnp.concatenate([a[:, sl], bm[:, sl]], axis=1)
        y2 = _dot(ab, w2_ref[g])
        o_ref[0, :, sl] = (y2[:, :FN_GDIM] * _silu(zd_ref[:, sl].astype(F32))).astype(BF16)
        r_ref[:, sl] = y2[:, FN_GDIM:].astype(BF16)
        carry_ref[:, sl] = y2[0:8, FN_GDIM:]
    mirror = _dot(rev_ref[...], r_ref[...])
    row0 = lax.broadcasted_iota(jnp.int32, (tk, BRANCH), 0) == 0
    mirror = jnp.where(row0, carry, mirror)
    o_ref[1] = (mirror * _silu(zm_ref[...].astype(F32))).astype(BF16)


def _fn_dft(mats, u, z, w2, tk):
    m, e, rev = mats
    b, s, _ = u.shape
    nt = s // 2 // tk
    return pl.pallas_call(
        functools.partial(_fn_dft_kernel, tk=tk),
        grid=(b, nt),
        in_specs=[pl.BlockSpec((None, 2 * tk, s), lambda bb, jj: (nt - 1 - jj, 0, 0)),
                  _full_spec((8, s)), _full_spec((tk, tk)),
                  pl.BlockSpec((None, s, BRANCH), lambda bb, jj: (bb, 0, 0)),
                  pl.BlockSpec((None, tk, BRANCH), lambda bb, jj: (bb, nt - 1 - jj, 0)),
                  pl.BlockSpec((None, tk, BRANCH), lambda bb, jj: (bb, nt + jj, 0)),
                  _full_spec((FN_GROUPS, 2 * FN_GDIM, 2 * FN_GDIM))],
        out_specs=pl.BlockSpec((None, 2, tk, BRANCH), lambda bb, jj: (bb, 0, nt - 1 - jj, 0)),
        out_shape=jax.ShapeDtypeStruct((b, 2, s // 2, BRANCH), BF16),
        scratch_shapes=[pltpu.VMEM((tk, BRANCH), BF16), pltpu.VMEM((8, BRANCH), F32)],
        compiler_params=_cparams(("parallel", "arbitrary")),
        name="fn_dft",
    )(m, e, rev, u, z, z, w2)


def _out_kernel(x_ref, o_ref, p_ref, wo_ref, gw_ref, pw_ref, *rest, final):
    y_ref = rest[-1]
    x1 = x_ref[...] + _dot(o_ref[...], wo_ref[...])
    gate = _sigmoid(_dot(x1.astype(BF16), gw_ref[...]))
    x2 = x1 + gate * _dot(p_ref[...].astype(BF16), pw_ref[...])
    if final:
        x2 = _rmsnorm(x2, rest[0][...])
    y_ref[...] = x2


def _out_proj(x, o, p, li, wo, gw, pw, final_g, tm):
    b, s, d = x.shape
    final = final_g is not None
    if o.ndim == 4:
        nt = s // 2 // tm
        assert o.shape == (b, 2, nt * tm, BRANCH)
        o_spec = pl.BlockSpec((None, None, tm, BRANCH),
                              lambda bb, i: (bb, i // nt, jnp.where(i < nt, i, 2 * nt - 1 - i), 0))
    else:
        o_spec = _tok_spec(tm, BRANCH)
    in_specs = [_tok_spec(tm, d), o_spec,
                pl.BlockSpec((None, None, tm, PLE_DIM), lambda bb, i: (li, bb, i, 0)),
                _full_spec((BRANCH, d)), _full_spec((d, d)), _full_spec((PLE_DIM, d))]
    args = [x, o, p, wo, gw, pw]
    if final:
        in_specs.append(_full_spec((1, d)))
        args.append(final_g)
    return pl.pallas_call(
        functools.partial(_out_kernel, final=final),
        grid=(b, s // tm),
        in_specs=in_specs,
        out_specs=_tok_spec(tm, d),
        out_shape=jax.ShapeDtypeStruct((b, s, d), F32),
        compiler_params=_cparams(("parallel", "parallel")),
        name="out_proj",
    )(*args)


def _na_in_kernel(x_ref, g_ref, w_ref, q_ref, k_ref, v_ref, z_ref):
    h = _rmsnorm(x_ref[...], g_ref[...]).astype(BF16)
    q_ref[...] = (_dot(h, w_ref[:, :BRANCH]) * (NA_HDIM ** -0.5 * LOG2E)).astype(BF16)
    k_ref[...] = _dot(h, w_ref[:, BRANCH:2 * BRANCH]).astype(BF16)
    v_ref[...] = _dot(h, w_ref[:, 2 * BRANCH:3 * BRANCH]).astype(BF16)
    z_ref[...] = _dot(h, w_ref[:, 3 * BRANCH:]).astype(BF16)


def _na_in(x, g, w, tm):
    b, s, d = x.shape
    tok = jax.ShapeDtypeStruct((b, s, BRANCH), BF16)
    return pl.pallas_call(
        _na_in_kernel,
        grid=(b, s // tm),
        in_specs=[_tok_spec(tm, d), _full_spec((1, d)), _full_spec((d, 4 * BRANCH))],
        out_specs=[_tok_spec(tm, BRANCH)] * 4,
        out_shape=[tok] * 4,
        compiler_params=_cparams(("parallel", "parallel")),
        name="na_in",
    )(x, g, w)


def _na_bias_table(rpb):
    d = np.arange(NA_KH)
    i = np.arange(NA_KH)
    ro = i[None, :] - d[:, None] + (NA_KH - 1)
    c = np.arange(GRID_W)
    kc = np.arange(GRID_W)
    win = np.clip(c - NA_KW // 2, 0, GRID_W - NA_KW)
    valid = (kc[:, None] >= win[None, :]) & (kc[:, None] < win[None, :] + NA_KW)
    pad = GRID_W - NA_KW
    r1 = jnp.pad(rpb[:, ro, :] * LOG2E, ((0, 0), (0, 0), (0, 0), (pad, pad)))
    cols = [r1[..., GRID_W - 1 - cc:2 * GRID_W - 1 - cc] for cc in range(GRID_W)]
    bias = jnp.stack(cols, axis=-1)
    bias = jnp.where(jnp.asarray(valid)[None, None, None], bias, NA_NEG)
    bias = bias.reshape(NA_GROUPS, NA_HPG, NA_KH, NA_KH, GRID_W, GRID_W)
    bias = jnp.transpose(bias, (0, 2, 3, 4, 1, 5))
    return bias.reshape(NA_GROUPS, NA_KH, NA_KH * GRID_W, NA_HPG * GRID_W).astype(F32)


def _na_attn_kernel(q_ref, k_ref, v_ref, z_ref, t_ref, o_ref, ot_ref, *, rows):
    lane = lax.broadcasted_iota(jnp.int32, (GRID_W, LANE), 1)
    head_masks = [(lane >= hh * NA_HDIM) & (lane < (hh + 1) * NA_HDIM) for hh in range(NA_HPG)]
    row_head = lax.broadcasted_iota(jnp.int32, (LANE, LANE), 0) // NA_HDIM
    low_heads = row_head < NA_HPG // 2
    even_heads = (row_head & 1) == 0
    nkeys = NA_KH * GRID_W

    def logits(r):
        rs = jnp.clip(r - NA_KH // 2, 0, rows - NA_KH)
        q_r = q_ref[pl.ds(pl.multiple_of(r * GRID_W, GRID_W), GRID_W), :]
        zero = jnp.zeros_like(q_r)
        qs = jnp.concatenate([jnp.where(m, q_r, zero) for m in head_masks], axis=0)
        kwin = k_ref[pl.ds(pl.multiple_of(rs * GRID_W, GRID_W), nkeys), :]
        return _dot_nt(kwin, qs) + t_ref[r - rs]

    def probs(sc):
        p = jnp.exp2(sc - jnp.max(sc, axis=0, keepdims=True))
        return p.astype(BF16), jnp.sum(p, axis=0, keepdims=True)

    def values(r, p, l):
        rs = jnp.clip(r - NA_KH // 2, 0, rows - NA_KH)
        vwin = v_ref[pl.ds(pl.multiple_of(rs * GRID_W, GRID_W), nkeys), :]
        return _dot_tn(vwin, p) * (1.0 / l)

    def emit(r, ot):
        w = jnp.where(low_heads, ot[:, :LANE], ot[:, LANE:])
        w = jnp.where(even_heads, w, pltpu.roll(w, GRID_W, axis=1))
        o = w.T[:GRID_W]
        q0 = pl.multiple_of(r * GRID_W, GRID_W)
        zz = z_ref[pl.ds(q0, GRID_W), :].astype(F32)
        o_ref[pl.ds(q0, GRID_W), :] = (o * _silu(zz)).astype(BF16)

    def step(i, emit_previous):
        rws = [i * NA_UNROLL + u for u in range(NA_UNROLL)]
        scs = [logits(r) for r in rws]
        if emit_previous:
            for u in range(NA_UNROLL):
                emit((i - 1) * NA_UNROLL + u, ot_ref[u])
        pls = [probs(sc) for sc in scs]
        for u, (r, (p, l)) in enumerate(zip(rws, pls)):
            ot_ref[u] = values(r, p, l)

    def body(i, carry):
        step(i, True)
        return carry

    nsteps = rows // NA_UNROLL
    step(0, False)
    lax.fori_loop(1, nsteps, body, 0)
    for u in range(NA_UNROLL):
        emit((nsteps - 1) * NA_UNROLL + u, ot_ref[u])


def _na_attn(q, k, v, z, table):
    b, s, _ = q.shape
    rows = s // GRID_W
    assert rows >= NA_KH
    grp = pl.BlockSpec((None, s, LANE), lambda g, bb: (bb, 0, g))
    return pl.pallas_call(
        functools.partial(_na_attn_kernel, rows=rows),
        grid=(NA_GROUPS, b),
        in_specs=[grp, grp, grp, grp,
                  pl.BlockSpec((None, NA_KH, NA_KH * GRID_W, NA_HPG * GRID_W),
                               lambda g, bb: (g, 0, 0, 0))],
        out_specs=grp,
        out_shape=jax.ShapeDtypeStruct((b, s, BRANCH), BF16),
        scratch_shapes=[pltpu.VMEM((NA_UNROLL, LANE, NA_HPG * GRID_W), F32)],
        compiler_params=_cparams(("parallel", "parallel")),
        name="na_attn",
    )(q, k, v, z, table)


def _mla_in_kernel(x_ref, g_ref, w1_ref, gq_ref, wuq_ref, gkv_ref, wuk_ref, wuvt_ref,
                   rq_ref, rk_ref, q_ref, k_ref, vt_ref, z_ref):
    h = _rmsnorm(x_ref[...], g_ref[...]).astype(BF16)
    c0, c1, c2, c3 = MLA_Q_LORA, MLA_Q_LORA + MLA_KV_LORA, MLA_Q_LORA + MLA_KV_LORA + LANE, \
        MLA_Q_LORA + MLA_KV_LORA + 2 * LANE
    cq = _dot(h, w1_ref[:, :c0])
    ckv = _dot(h, w1_ref[:, c0:c1])
    kp = _dot(h, w1_ref[:, c1:c2])
    kr = _dot(h, w1_ref[:, c2:c3])
    z_ref[...] = _dot(h, w1_ref[:, c3:]).astype(BF16)
    cqn = _rmsnorm(cq, gq_ref[...]).astype(BF16)
    ckvn = _rmsnorm(ckv, gkv_ref[...]).astype(BF16)
    rq = rq_ref[...]
    for hh in range(MLA_HEADS):
        sl = slice(hh * MLA_QK, (hh + 1) * MLA_QK)
        q_ref[:, sl] = (_dot(cqn, wuq_ref[:, sl]) * rq).astype(BF16)
    krr = (kp * rk_ref[:, :LANE] + kr * rk_ref[:, LANE:]).astype(BF16)
    kn = _dot(ckvn, wuk_ref[...]).astype(BF16)
    for hh in range(MLA_HEADS):
        k_ref[:, hh * MLA_QK:hh * MLA_QK + MLA_NOPE] = kn[:, hh * MLA_NOPE:(hh + 1) * MLA_NOPE]
        k_ref[:, hh * MLA_QK + MLA_NOPE:(hh + 1) * MLA_QK] = krr
    vt_ref[...] = _dot_nt(wuvt_ref[...], ckvn).astype(BF16)


def _mla_in(x, g, w1, gq, wuq, gkv, wuk, wuvt, rq, rk, tm):
    b, s, d = x.shape
    n1 = w1.shape[1]
    qk = MLA_HEADS * MLA_QK
    rope_spec = pl.BlockSpec((tm, 2 * LANE), lambda bb, i: (i, 0))
    return pl.pallas_call(
        _mla_in_kernel,
        grid=(b, s // tm),
        in_specs=[_tok_spec(tm, d), _full_spec((1, d)), _full_spec((d, n1)),
                  _full_spec((1, MLA_Q_LORA)), _full_spec((MLA_Q_LORA, qk)),
                  _full_spec((1, MLA_KV_LORA)), _full_spec((MLA_KV_LORA, BRANCH)),
                  _full_spec((BRANCH, MLA_KV_LORA)), rope_spec, rope_spec],
        out_specs=[_tok_spec(tm, qk), _tok_spec(tm, qk),
                   pl.BlockSpec((None, BRANCH, tm), lambda bb, i: (bb, 0, i)), _tok_spec(tm, BRANCH)],
        out_shape=[jax.ShapeDtypeStruct((b, s, qk), BF16), jax.ShapeDtypeStruct((b, s, qk), BF16),
                   jax.ShapeDtypeStruct((b, BRANCH, s), BF16), jax.ShapeDtypeStruct((b, s, BRANCH), BF16)],
        compiler_params=_cparams(("parallel", "parallel")),
        name="mla_in",
    )(x, g, w1, gq, wuq, gkv, wuk, wuvt, rq, rk)


def _mla_attn_kernel(q_ref, k_ref, vt_ref, z_ref, o_ref, *, tq, tk, nk, nq):
    items = [(t, j) for t in range(nq) for j in range(nk)]
    rows = lambda t: slice(t * tq, (t + 1) * tq)
    logits = lambda t, j: _dot_nt(k_ref[j * tk:(j + 1) * tk, :], q_ref[rows(t), :])
    st_next = logits(*items[0])
    for idx, (t, j) in enumerate(items):
        st = st_next
        if idx + 1 < len(items):
            st_next = logits(*items[idx + 1])
        if j == 0:
            m = jnp.full((1, tq), -jnp.inf, F32)
            l = jnp.zeros((1, tq), F32)
            acc = jnp.zeros((MLA_V, tq), F32)
        m_new = jnp.maximum(m, jnp.max(st, axis=0, keepdims=True))
        alpha = jnp.exp2(m - m_new)
        p = jnp.exp2(st - m_new)
        l = alpha * l + jnp.sum(p, axis=0, keepdims=True)
        acc = alpha * acc + _dot(vt_ref[:, j * tk:(j + 1) * tk], p.astype(BF16))
        m = m_new
        if j == nk - 1:
            o = (acc * (1.0 / l)).T
            o_ref[rows(t), :] = (o * _silu(z_ref[rows(t), :].astype(F32))).astype(BF16)


def _mla_attn(q, k, vt, z, tq, tk):
    b, _, s = vt.shape
    nq = min(MLA_QTILES, s // tq)
    tb = nq * tq
    return pl.pallas_call(
        functools.partial(_mla_attn_kernel, tq=tq, tk=tk, nk=s // tk, nq=nq),
        grid=(b, MLA_HEADS, s // tb),
        in_specs=[pl.BlockSpec((None, tb, MLA_QK), lambda bb, h, i: (bb, i, h)),
                  pl.BlockSpec((None, s, MLA_QK), lambda bb, h, i: (bb, 0, h)),
                  pl.BlockSpec((None, MLA_V, s), lambda bb, h, i: (bb, h, 0)),
                  pl.BlockSpec((None, tb, MLA_V), lambda bb, h, i: (bb, i, h))],
        out_specs=pl.BlockSpec((None, tb, MLA_V), lambda bb, h, i: (bb, i, h)),
        out_shape=jax.ShapeDtypeStruct((b, s, BRANCH), BF16),
        compiler_params=_cparams(("parallel", "parallel", "arbitrary")),
        name="mla_attn",
    )(q, k, vt, z)


def _hg_in_kernel(x_ref, g_ref, w_ref, lbr_ref, q_ref, lf_ref, lb_ref, i_ref, z_ref, *, li):
    h = _rmsnorm(x_ref[...], g_ref[...]).astype(BF16)
    raw = lbr_ref[...]
    e = jnp.exp(raw - jnp.max(raw, axis=0, keepdims=True))
    sm = e / jnp.sum(e, axis=0, keepdims=True)
    lower = jnp.zeros_like(sm[0])
    for dd in range(1, li + 1):
        lower = lower + sm[dd]
    q_ref[...] = _silu(_dot(h, w_ref[:, :BRANCH])).astype(BF16)
    for idx, ref in ((0, lf_ref), (1, lb_ref)):
        lb = lower[idx:idx + 1]
        raw_f = _dot(h, w_ref[:, (1 + idx) * BRANCH:(2 + idx) * BRANCH])
        ref[...] = jnp.log(lb + (1.0 - lb) * _sigmoid(raw_f)) * LOG2E
    i_ref[...] = _dot(h, w_ref[:, 3 * BRANCH:4 * BRANCH]).astype(BF16)
    z_ref[...] = _dot(h, w_ref[:, 4 * BRANCH:]).astype(BF16)


def _hg_in(x, g, w, lb_raw, li, tm):
    b, s, d = x.shape
    bf = jax.ShapeDtypeStruct((b, s, BRANCH), BF16)
    f32 = jax.ShapeDtypeStruct((b, s, BRANCH), F32)
    return pl.pallas_call(
        functools.partial(_hg_in_kernel, li=li),
        grid=(b, s // tm),
        in_specs=[_tok_spec(tm, d), _full_spec((1, d)), _full_spec((d, 5 * BRANCH)),
                  _full_spec((DEPTH, 2, BRANCH))],
        out_specs=[_tok_spec(tm, BRANCH)] * 5,
        out_shape=[bf, f32, f32, bf, bf],
        compiler_params=_cparams(("parallel", "parallel")),
        name="hg_in",
    )(x, g, w, lb_raw)


def _hg_level_operands(q, k, lf2, cum, n, fwd, sel):
    L = HG_CHUNK
    half = n // 2
    ref_row = half - 1 if fwd else half
    if n >= 16:
        zero = jnp.zeros((half, HG_DK), BF16)
        qt_parts, kt_parts = [], []
        for blk in range(L // n):
            cm = cum[blk * n + ref_row:blk * n + ref_row + 1]
            for is_upper in (False, True):
                r0 = blk * n + (half if is_upper else 0)
                seg = slice(r0, r0 + half)
                e = jnp.exp2(-jnp.abs(cum[seg] - cm))
                if is_upper == fwd:
                    qt_parts.append((q[seg] * e).astype(BF16))
                    kt_parts.append(zero)
                else:
                    qt_parts.append(zero)
                    kt_parts.append((k[seg] * e).astype(BF16))
        return jnp.concatenate(qt_parts, axis=0), jnp.concatenate(kt_parts, axis=0)
    if n == 2:
        q_side = sel["upper"][n] if fwd else jnp.logical_not(sel["upper"][n])
        gl = jnp.where(q_side, lf2, 0.0)
    else:
        c3 = cum.reshape(L // n, n, HG_DK)
        cm = jnp.broadcast_to(c3[:, ref_row:ref_row + 1, :], c3.shape).reshape(L, HG_DK)
        gl = -jnp.abs(cum - cm)
    e = jnp.exp2(gl)
    return (q * e).astype(BF16), (k * e).astype(BF16)


def _hg_chunks(items, sel):
    L = HG_CHUNK
    cats = []
    for q, lf2, v, tri3, fwd in items:
        hi = lf2.astype(BF16)
        r1 = lf2 - hi.astype(F32)
        mid = r1.astype(BF16)
        lo = (r1 - mid.astype(F32)).astype(BF16)
        cats.append(jnp.concatenate([hi, mid, lo], axis=0))
    cums = [_dot(it[3], cat) for it, cat in zip(items, cats)]
    ks = [1.0 - jnp.exp2(it[1]) for it in items]
    accs = [jnp.where(sel["eye"], jnp.sum(it[0] * k, axis=-1, keepdims=True), 0.0)
            for it, k in zip(items, ks)]
    for n in HG_LEVELS:
        ops = [_hg_level_operands(it[0], k, it[1], cum, n, it[4], sel)
               for it, k, cum in zip(items, ks, cums)]
        prods = [_dot_nt(qt, kt) for qt, kt in ops]
        accs = [jnp.where(sel["own"][it[4]][n], p, a) for it, p, a in zip(items, prods, accs)]
    o_intras = [_dot(a.astype(BF16), it[2]) for a, it in zip(accs, items)]
    edges = [cum[L - 1:L] if it[4] else cum[0:1] for it, cum in zip(items, cums)]
    khs = [(k * jnp.exp2(e - cum)).astype(BF16) for k, e, cum in zip(ks, edges, cums)]
    us = [_dot_tn(it[2], kh) for it, kh in zip(items, khs)]
    qbs = [(it[0] * jnp.exp2(cum)).astype(BF16) for it, cum in zip(items, cums)]
    return [(o, qb, u, jnp.exp2(e)) for o, qb, u, e in zip(o_intras, qbs, us, edges)]


def _hg_scan_kernel(q_ref, lf_ref, lb_ref, v_ref, z_ref, g_ref, o_ref,
                    of_ref, ob_ref, sf_ref, sb_ref, *, nchunks, fin_rows):
    L = HG_CHUNK
    t_i = lax.broadcasted_iota(jnp.int32, (L, L), 0)
    s_i = lax.broadcasted_iota(jnp.int32, (L, L), 1)
    x_i = t_i ^ s_i
    row = lax.broadcasted_iota(jnp.int32, (L, HG_DK), 0)
    sel = {"eye": t_i == s_i,
           "own": {True: {n: (x_i < n) & (x_i >= n // 2) & (t_i > s_i) for n in HG_LEVELS},
                   False: {n: (x_i < n) & (x_i >= n // 2) & (t_i < s_i) for n in HG_LEVELS}},
           "upper": {n: (row & (n - 1)) >= n // 2 for n in HG_LEVELS if n < 16}}
    t3 = lax.broadcasted_iota(jnp.int32, (L, 3 * L), 0)
    u3 = lax.broadcasted_iota(jnp.int32, (L, 3 * L), 1) & (L - 1)
    tri_f = (u3 <= t3).astype(BF16)
    tri_b = (u3 >= t3).astype(BF16)
    sf_ref[...] = jnp.zeros_like(sf_ref)
    sb_ref[...] = jnp.zeros_like(sb_ref)

    def body(i, carry):
        dirs = ((True, lf_ref, sf_ref, of_ref, tri_f), (False, lb_ref, sb_ref, ob_ref, tri_b))
        items, rows_of = [], []
        for fwd, gate_ref, _, _, tri in dirs:
            for jj in range(HG_UNROLL):
                c = i * HG_UNROLL + jj
                c = c if fwd else nchunks - 1 - c
                rows = pl.ds(pl.multiple_of(c * L, L), L)
                rows_of.append(rows)
                items.append((q_ref[rows, :].astype(F32), gate_ref[rows, :], v_ref[rows, :], tri, fwd))
        parts = _hg_chunks(items, sel)
        sts = [d[2][...] for d in dirs]
        for jj in range(HG_UNROLL):
            for di, (_, _, _, out_ref, _) in enumerate(dirs):
                o_intra, qb, u, dec = parts[di * HG_UNROLL + jj]
                out_ref[rows_of[di * HG_UNROLL + jj], :] = o_intra + _dot_nt(qb, sts[di].astype(BF16))
                sts[di] = sts[di] * dec + u
        for d, st in zip(dirs, sts):
            d[2][...] = st
        return carry

    lax.fori_loop(0, nchunks // HG_UNROLL, body, 0)

    def fin(i, carry):
        rows = pl.ds(pl.multiple_of(i * fin_rows, fin_rows), fin_rows)
        o = of_ref[rows, :] + ob_ref[rows, :]
        y = _rmsnorm(o, g_ref[...])
        o_ref[rows, :] = (y * _silu(z_ref[rows, :].astype(F32))).astype(BF16)
        return carry

    lax.fori_loop(0, (nchunks * L) // fin_rows, fin, 0)


def _hg_scan(q, lf, lb, v, z, g_out):
    b, s, _ = q.shape
    head = pl.BlockSpec((None, s, HG_DK), lambda bb, h: (bb, 0, h))
    fin_rows = min(256, s)
    return pl.pallas_call(
        functools.partial(_hg_scan_kernel, nchunks=s // HG_CHUNK, fin_rows=fin_rows),
        grid=(b, HG_HEADS),
        in_specs=[head] * 5 + [pl.BlockSpec((1, HG_DK), lambda bb, h: (0, h))],
        out_specs=head,
        out_shape=jax.ShapeDtypeStruct((b, s, BRANCH), BF16),
        scratch_shapes=[pltpu.VMEM((s, HG_DK), F32), pltpu.VMEM((s, HG_DK), F32),
                        pltpu.VMEM((HG_DK, HG_DK), F32), pltpu.VMEM((HG_DK, HG_DK), F32)],
        compiler_params=_cparams(("parallel", "parallel")),
        name="hg_scan",
    )(q, lf, lb, v, z, g_out)


def _rot_half_cols(w):
    half = MLA_ROPE // 2
    return jnp.concatenate([-w[..., half:], w[..., :half]], axis=-1)


def _prep(s, norm_g, fn_w_in, fn_w_mix, fn_w_out, na_w_in, na_rpb, na_w_out,
          mla_w_in, mla_g_q, mla_w_uq, mla_g_kv, mla_w_ukv, mla_w_out,
          hg_w_in, hg_lb_raw, hg_g_out, hg_w_out, ple_w, ple_gate_w, final_g, tk):
    bf = lambda t: t.astype(BF16)
    row = lambda t: t.reshape(1, -1).astype(F32)
    w = {}
    w["norm_g"] = [row(norm_g[i]) for i in range(DEPTH)]
    w["final_g"] = row(final_g)
    w["ple_w"] = [bf(ple_w[i]) for i in range(DEPTH)]
    w["gate_w"] = [bf(ple_gate_w[i]) for i in range(DEPTH)]
    w["w_out"] = [bf(fn_w_out[0]), bf(na_w_out[0]), bf(mla_w_out[0]), bf(hg_w_out[0])]
    w["fn_w_in"] = bf(fn_w_in[0])
    w["fn_w2"] = _fn_fold(fn_w_mix[0].astype(F32), s)
    w["dft"] = _dft_mats(s, tk)
    w["na_w_in"] = bf(na_w_in[0])
    w["na_table"] = _na_bias_table(na_rpb[0].astype(F32))
    wi = mla_w_in[0]
    c0, c1, c2 = MLA_Q_LORA, MLA_Q_LORA + MLA_KV_LORA, MLA_Q_LORA + MLA_KV_LORA + MLA_ROPE
    wkpe = wi[:, c1:c2]
    wrot = _rot_half_cols(wkpe)
    w["mla_w1"] = bf(jnp.concatenate([wi[:, :c1], wkpe, wkpe, wrot, wrot, wi[:, c2:]], axis=1))
    wq = mla_w_uq[0].reshape(MLA_Q_LORA, MLA_HEADS, MLA_NOPE + MLA_ROPE)
    wq_pe = wq[..., MLA_NOPE:]
    w["mla_wuq"] = bf(jnp.concatenate([wq[..., :MLA_NOPE], wq_pe, _rot_half_cols(wq_pe)], axis=-1)
                      .reshape(MLA_Q_LORA, MLA_HEADS * MLA_QK))
    wkv = mla_w_ukv[0].reshape(MLA_KV_LORA, MLA_HEADS, MLA_NOPE + MLA_V)
    w["mla_wuk"] = bf(wkv[..., :MLA_NOPE].reshape(MLA_KV_LORA, BRANCH))
    w["mla_wuvt"] = bf(wkv[..., MLA_NOPE:].reshape(MLA_KV_LORA, BRANCH).T)
    w["mla_gq"] = row(mla_g_q[0])
    w["mla_gkv"] = row(mla_g_kv[0])
    half = MLA_ROPE // 2
    inv = ROPE_THETA ** (-jnp.arange(half, dtype=F32) / half)
    ang = jnp.arange(s, dtype=F32)[:, None] * inv[None, :]
    cos, sin = jnp.cos(ang), jnp.sin(ang)
    cos64 = jnp.concatenate([cos, cos], axis=-1)
    sin64 = jnp.concatenate([sin, sin], axis=-1)
    scale = (MLA_NOPE + MLA_ROPE) ** -0.5 * math.log2(math.e)
    w["mla_rq"] = scale * jnp.concatenate([jnp.ones((s, MLA_NOPE), F32), cos64, sin64], axis=-1)
    w["mla_rk"] = jnp.concatenate([cos64, cos64, sin64, sin64], axis=-1)
    w["hg_w_in"] = bf(hg_w_in[0])
    w["hg_lb_raw"] = hg_lb_raw.astype(F32)
    w["hg_g_out"] = row(hg_g_out[0])
    return w


def _trunk(x, p, w, tm, tk, tq, tkv):
    tm_big = min(2 * tm, x.shape[1])

    def finish(x, o, li, final=False):
        return _out_proj(x, o, p, li, w["w_out"][li], w["gate_w"][li], w["ple_w"][li],
                         w["final_g"] if final else None, tm if o.ndim == 4 else tm_big)

    u, z = _fn_in(x, w["norm_g"][0], w["fn_w_in"], tm_big)
    o = _fn_dft(w["dft"], u, z, w["fn_w2"], tk)
    x = finish(x, o, 0)
    q, k, v, z = _na_in(x, w["norm_g"][1], w["na_w_in"], tm_big)
    o = _na_attn(q, k, v, z, w["na_table"])
    x = finish(x, o, 1)
    q, k, vt, z = _mla_in(x, w["norm_g"][2], w["mla_w1"], w["mla_gq"], w["mla_wuq"], w["mla_gkv"],
                          w["mla_wuk"], w["mla_wuvt"], w["mla_rq"], w["mla_rk"], tm)
    o = _mla_attn(q, k, vt, z, tq, tkv)
    x = finish(x, o, 2)
    q, lf, lb, v, z = _hg_in(x, w["norm_g"][3], w["hg_w_in"], w["hg_lb_raw"], 3, tm)
    o = _hg_scan(q, lf, lb, v, z, w["hg_g_out"])
    return finish(x, o, 3, final=True)


def kernel(x_prompt, x_sample, p_prompt, p_sample, norm_g, fn_w_in, fn_w_mix, fn_w_out, na_w_in, na_rpb, na_w_out, mla_w_in, mla_g_q, mla_w_uq, mla_g_kv, mla_w_ukv, mla_w_out, hg_w_in, hg_lb_raw, hg_g_out, hg_w_out, ple_w, ple_gate_w, final_g):
    s = x_prompt.shape[1]
    assert x_sample.shape[1] == s
    tm = min(512, s)
    tk = min(512, s)
    tq = min(1024, s)
    tkv = min(1024, s)
    w = _prep(s, norm_g, fn_w_in, fn_w_mix, fn_w_out, na_w_in, na_rpb, na_w_out,
              mla_w_in, mla_g_q, mla_w_uq, mla_g_kv, mla_w_ukv, mla_w_out,
              hg_w_in, hg_lb_raw, hg_g_out, hg_w_out, ple_w, ple_gate_w, final_g, tk)
    y_prompt = _trunk(x_prompt, p_prompt, w, tm, tk, tq, tkv)
    y_sample = _trunk(x_sample, p_sample, w, tm, tk, tq, tkv)
    return (y_prompt, y_sample)
```

```python
import functools
import math

import numpy as np
import jax
import jax.numpy as jnp
from jax import lax
from jax.experimental import pallas as pl
from jax.experimental.pallas import tpu as pltpu

F32 = jnp.float32
BF16 = jnp.bfloat16

D_MODEL = 1024
DEPTH = 4
BRANCH = 1024
PLE_DIM = 256
GRID_W = 64
EPS = 1e-6
LANE = 128
VMEM_LIMIT = 56 * 1024 * 1024

FN_GROUPS = 8
FN_GDIM = BRANCH // FN_GROUPS

NA_HEADS = 32
NA_HDIM = BRANCH // NA_HEADS
NA_KH = 8
NA_KW = 16
NA_HPG = LANE // NA_HDIM
NA_GROUPS = NA_HEADS // NA_HPG
NA_NEG = -1e30

MLA_HEADS = 8
MLA_NOPE = 128
MLA_ROPE = 64
MLA_V = BRANCH // MLA_HEADS
MLA_Q_LORA = 384
MLA_KV_LORA = 256
MLA_QK = MLA_NOPE + 2 * MLA_ROPE
MLA_QTILES = 4
ROPE_THETA = 10000.0

HG_HEADS = 8
HG_DK = BRANCH // HG_HEADS
HG_CHUNK = 64
HG_LEVELS = (64, 32, 16, 8, 4, 2)
HG_UNROLL = 16
NA_UNROLL = 16
LOG2E = math.log2(math.e)


def _cparams(sem):
    return pltpu.CompilerParams(dimension_semantics=sem, vmem_limit_bytes=VMEM_LIMIT)


def _rmsnorm(x, g):
    ms = jnp.mean(x * x, axis=-1, keepdims=True)
    return x * lax.rsqrt(ms + EPS) * g


def _sigmoid(x):
    return 1.0 / (1.0 + jnp.exp(-x))


def _silu(x):
    return x * _sigmoid(x)


def _dot(a, b):
    return jnp.dot(a, b, preferred_element_type=F32)


def _dot_nt(a, b):
    return lax.dot_general(a, b, (((1,), (1,)), ((), ())), preferred_element_type=F32)


def _dot_tn(a, b):
    return lax.dot_general(a, b, (((0,), (0,)), ((), ())), preferred_element_type=F32)


def _tok_spec(tm, width):
    return pl.BlockSpec((None, tm, width), lambda b, i: (b, i, 0))


def _full_spec(shape):
    nd = len(shape)
    return pl.BlockSpec(shape, lambda *_: (0,) * nd)


def _fn_in_kernel(x_ref, g_ref, w_ref, u_ref, z_ref):
    h = _rmsnorm(x_ref[...], g_ref[...]).astype(BF16)
    u_ref[...] = _dot(h, w_ref[:, :BRANCH]).astype(BF16)
    z_ref[...] = _dot(h, w_ref[:, BRANCH:]).astype(BF16)


def _fn_in(x, g, w, tm):
    b, s, d = x.shape
    return pl.pallas_call(
        _fn_in_kernel,
        grid=(b, s // tm),
        in_specs=[_tok_spec(tm, d), _full_spec((1, d)), _full_spec((d, 2 * BRANCH))],
        out_specs=[_tok_spec(tm, BRANCH)] * 2,
        out_shape=[jax.ShapeDtypeStruct((b, s, BRANCH), BF16)] * 2,
        compiler_params=_cparams(("parallel", "parallel")),
        name="fn_in",
    )(x, g, w)


def _fn_fold_kernel(c_ref, s_ref, w_ref, o_ref, *, scale):
    w = w_ref[...]
    hi = lax.Precision.HIGHEST
    wc = (scale * jnp.dot(c_ref[...], w, precision=hi, preferred_element_type=F32)).astype(BF16)
    ws = (scale * jnp.dot(s_ref[...], w, precision=hi, preferred_element_type=F32)).astype(BF16)
    o_ref[:FN_GDIM, :FN_GDIM] = wc
    o_ref[:FN_GDIM, FN_GDIM:] = wc
    o_ref[FN_GDIM:, :FN_GDIM] = -ws
    o_ref[FN_GDIM:, FN_GDIM:] = ws


def _fn_fold(w_mix, s):
    c = jnp.arange(FN_GDIM, dtype=jnp.int32)
    ang = ((c[:, None] * c[None, :]) % FN_GDIM).astype(F32) * (2.0 * math.pi / FN_GDIM)
    scale = 1.0 / math.sqrt(s * FN_GDIM)
    gspec = pl.BlockSpec((None, FN_GDIM, FN_GDIM), lambda g: (g, 0, 0))
    return pl.pallas_call(
        functools.partial(_fn_fold_kernel, scale=scale),
        grid=(FN_GROUPS,),
        in_specs=[_full_spec((FN_GDIM, FN_GDIM))] * 2 + [gspec],
        out_specs=pl.BlockSpec((None, 2 * FN_GDIM, 2 * FN_GDIM), lambda g: (g, 0, 0)),
        out_shape=jax.ShapeDtypeStruct((FN_GROUPS, 2 * FN_GDIM, 2 * FN_GDIM), BF16),
        name="fn_fold",
    )(jnp.cos(ang), jnp.sin(ang), w_mix)


def _dft_mats(s, tk):
    nt = s // 2 // tk
    k = jnp.arange(s // 2, dtype=jnp.int32)
    pos = jnp.arange(s, dtype=jnp.int32)
    ang = ((k[:, None] * pos[None, :]) % s).astype(F32) * (2.0 * math.pi / s)
    c = jnp.cos(ang).astype(BF16).reshape(nt, tk, s)
    sn = jnp.sin(ang).astype(BF16).reshape(nt, tk, s)
    nyq = jnp.where(pos % 2 == 0, 1.0, -1.0).astype(BF16)
    e = jnp.concatenate([nyq[None, :], jnp.zeros((7, s), BF16)], axis=0)
    r = np.arange(tk)
    rev = np.zeros((tk, tk), np.float32)
    rev[r[1:], tk - r[1:]] = 1.0
    return jnp.concatenate([c, sn], axis=1), e, jnp.asarray(rev, BF16)


def _fn_dft_kernel(m_ref, e_ref, rev_ref, u_ref, zd_ref, zm_ref, w2_ref, o_ref, r_ref, carry_ref, *, tk):
    u = u_ref[...]

    @pl.when(pl.program_id(1) == 0)
    def _():
        ev = _dot(e_ref[...], u).astype(BF16)
        for g in range(FN_GROUPS):
            sl = slice(g * FN_GDIM, (g + 1) * FN_GDIM)
            ab = jnp.concatenate([ev[:, sl], jnp.zeros_like(ev[:, sl])], axis=1)
            carry_ref[:, sl] = _dot(ab, w2_ref[g])[:, FN_GDIM:]

    r = _dot(m_ref[...], u)
    a = r[:tk].astype(BF16)
    bm = r[tk:].astype(BF16)
    carry = carry_ref[0:1, :]
    for g in range(FN_GROUPS):
        sl = slice(g * FN_GDIM, (g + 1) * FN_GDIM)
        ab = jnp.concatenate([a[:, sl], bm[:, sl]], axis=1)
        y2 = _dot(ab, w2_ref[g])
        o_ref[0, :, sl] = (y2[:, :FN_GDIM] * _silu(zd_ref[:, sl].astype(F32))).astype(BF16)
        r_ref[:, sl] = y2[:, FN_GDIM:].astype(BF16)
        carry_ref[:, sl] = y2[0:8, FN_GDIM:]
    mirror = _dot(rev_ref[...], r_ref[...])
    row0 = lax.broadcasted_iota(jnp.int32, (tk, BRANCH), 0) == 0
    mirror = jnp.where(row0, carry, mirror)
    o_ref[1] = (mirror * _silu(zm_ref[...].astype(F32))).astype(BF16)


def _fn_dft(mats, u, z, w2, tk):
    m, e, rev = mats
    b, s, _ = u.shape
    nt = s // 2 // tk
    return pl.pallas_call(
        functools.partial(_fn_dft_kernel, tk=tk),
        grid=(b, nt),
        in_specs=[pl.BlockSpec((None, 2 * tk, s), lambda bb, jj: (nt - 1 - jj, 0, 0)),
                  _full_spec((8, s)), _full_spec((tk, tk)),
                  pl.BlockSpec((None, s, BRANCH), lambda bb, jj: (bb, 0, 0)),
                  pl.BlockSpec((None, tk, BRANCH), lambda bb, jj: (bb, nt - 1 - jj, 0)),
                  pl.BlockSpec((None, tk, BRANCH), lambda bb, jj: (bb, nt + jj, 0)),
                  _full_spec((FN_GROUPS, 2 * FN_GDIM, 2 * FN_GDIM))],
        out_specs=pl.BlockSpec((None, 2, tk, BRANCH), lambda bb, jj: (bb, 0, nt - 1 - jj, 0)),
        out_shape=jax.ShapeDtypeStruct((b, 2, s // 2, BRANCH), BF16),
        scratch_shapes=[pltpu.VMEM((tk, BRANCH), BF16), pltpu.VMEM((8, BRANCH), F32)],
        compiler_params=_cparams(("parallel", "arbitrary")),
        name="fn_dft",
    )(m, e, rev, u, z, z, w2)


def _out_kernel(x_ref, o_ref, p_ref, wo_ref, gw_ref, pw_ref, *rest, final):
    y_ref = rest[-1]
    x1 = x_ref[...] + _dot(o_ref[...], wo_ref[...])
    gate = _sigmoid(_dot(x1.astype(BF16), gw_ref[...]))
    x2 = x1 + gate * _dot(p_ref[...].astype(BF16), pw_ref[...])
    if final:
        x2 = _rmsnorm(x2, rest[0][...])
    y_ref[...] = x2


def _out_proj(x, o, p, li, wo, gw, pw, final_g, tm):
    b, s, d = x.shape
    final = final_g is not None
    if o.ndim == 4:
        nt = s // 2 // tm
        assert o.shape == (b, 2, nt * tm, BRANCH)
        o_spec = pl.BlockSpec((None, None, tm, BRANCH),
                              lambda bb, i: (bb, i // nt, jnp.where(i < nt, i, 2 * nt - 1 - i), 0))
    else:
        o_spec = _tok_spec(tm, BRANCH)
    in_specs = [_tok_spec(tm, d), o_spec,
                pl.BlockSpec((None, None, tm, PLE_DIM), lambda bb, i: (li, bb, i, 0)),
                _full_spec((BRANCH, d)), _full_spec((d, d)), _full_spec((PLE_DIM, d))]
    args = [x, o, p, wo, gw, pw]
    if final:
        in_specs.append(_full_spec((1, d)))
        args.append(final_g)
    return pl.pallas_call(
        functools.partial(_out_kernel, final=final),
        grid=(b, s // tm),
        in_specs=in_specs,
        out_specs=_tok_spec(tm, d),
        out_shape=jax.ShapeDtypeStruct((b, s, d), F32),
        compiler_params=_cparams(("parallel", "parallel")),
        name="out_proj",
    )(*args)


def _na_in_kernel(x_ref, g_ref, w_ref, q_ref, k_ref, v_ref, z_ref):
    h = _rmsnorm(x_ref[...], g_ref[...]).astype(BF16)
    q_ref[...] = (_dot(h, w_ref[:, :BRANCH]) * (NA_HDIM ** -0.5 * LOG2E)).astype(BF16)
    k_ref[...] = _dot(h, w_ref[:, BRANCH:2 * BRANCH]).astype(BF16)
    v_ref[...] = _dot(h, w_ref[:, 2 * BRANCH:3 * BRANCH]).astype(BF16)
    z_ref[...] = _dot(h, w_ref[:, 3 * BRANCH:]).astype(BF16)


def _na_in(x, g, w, tm):
    b, s, d = x.shape
    tok = jax.ShapeDtypeStruct((b, s, BRANCH), BF16)
    return pl.pallas_call(
        _na_in_kernel,
        grid=(b, s // tm),
        in_specs=[_tok_spec(tm, d), _full_spec((1, d)), _full_spec((d, 4 * BRANCH))],
        out_specs=[_tok_spec(tm, BRANCH)] * 4,
        out_shape=[tok] * 4,
        compiler_params=_cparams(("parallel", "parallel")),
        name="na_in",
    )(x, g, w)


def _na_bias_table(rpb):
    d = np.arange(NA_KH)
    i = np.arange(NA_KH)
    ro = i[None, :] - d[:, None] + (NA_KH - 1)
    c = np.arange(GRID_W)
    kc = np.arange(GRID_W)
    win = np.clip(c - NA_KW // 2, 0, GRID_W - NA_KW)
    valid = (kc[:, None] >= win[None, :]) & (kc[:, None] < win[None, :] + NA_KW)
    pad = GRID_W - NA_KW
    r1 = jnp.pad(rpb[:, ro, :] * LOG2E, ((0, 0), (0, 0), (0, 0), (pad, pad)))
    cols = [r1[..., GRID_W - 1 - cc:2 * GRID_W - 1 - cc] for cc in range(GRID_W)]
    bias = jnp.stack(cols, axis=-1)
    bias = jnp.where(jnp.asarray(valid)[None, None, None], bias, NA_NEG)
    bias = bias.reshape(NA_GROUPS, NA_HPG, NA_KH, NA_KH, GRID_W, GRID_W)
    bias = jnp.transpose(bias, (0, 2, 3, 4, 1, 5))
    return bias.reshape(NA_GROUPS, NA_KH, NA_KH * GRID_W, NA_HPG * GRID_W).astype(F32)


def _na_attn_kernel(q_ref, k_ref, v_ref, z_ref, t_ref, o_ref, ot_ref, *, rows):
    lane = lax.broadcasted_iota(jnp.int32, (GRID_W, LANE), 1)
    head_masks = [(lane >= hh * NA_HDIM) & (lane < (hh + 1) * NA_HDIM) for hh in range(NA_HPG)]
    row_head = lax.broadcasted_iota(jnp.int32, (LANE, LANE), 0) // NA_HDIM
    low_heads = row_head < NA_HPG // 2
    even_heads = (row_head & 1) == 0
    nkeys = NA_KH * GRID_W

    def logits(r):
        rs = jnp.clip(r - NA_KH // 2, 0, rows - NA_KH)
        q_r = q_ref[pl.ds(pl.multiple_of(r * GRID_W, GRID_W), GRID_W), :]
        zero = jnp.zeros_like(q_r)
        qs = jnp.concatenate([jnp.where(m, q_r, zero) for m in head_masks], axis=0)
        kwin = k_ref[pl.ds(pl.multiple_of(rs * GRID_W, GRID_W), nkeys), :]
        return _dot_nt(kwin, qs) + t_ref[r - rs]

    def probs(sc):
        p = jnp.exp2(sc - jnp.max(sc, axis=0, keepdims=True))
        return p.astype(BF16), jnp.sum(p, axis=0, keepdims=True)

    def values(r, p, l):
        rs = jnp.clip(r - NA_KH // 2, 0, rows - NA_KH)
        vwin = v_ref[pl.ds(pl.multiple_of(rs * GRID_W, GRID_W), nkeys), :]
        return _dot_tn(vwin, p) * (1.0 / l)

    def emit(r, ot):
        w = jnp.where(low_heads, ot[:, :LANE], ot[:, LANE:])
        w = jnp.where(even_heads, w, pltpu.roll(w, GRID_W, axis=1))
        o = w.T[:GRID_W]
        q0 = pl.multiple_of(r * GRID_W, GRID_W)
        zz = z_ref[pl.ds(q0, GRID_W), :].astype(F32)
        o_ref[pl.ds(q0, GRID_W), :] = (o * _silu(zz)).astype(BF16)

    def step(i, emit_previous):
        rws = [i * NA_UNROLL + u for u in range(NA_UNROLL)]
        scs = [logits(r) for r in rws]
        if emit_previous:
            for u in range(NA_UNROLL):
                emit((i - 1) * NA_UNROLL + u, ot_ref[u])
        pls = [probs(sc) for sc in scs]
        for u, (r, (p, l)) in enumerate(zip(rws, pls)):
            ot_ref[u] = values(r, p, l)

    def body(i, carry):
        step(i, True)
        return carry

    nsteps = rows // NA_UNROLL
    step(0, False)
    lax.fori_loop(1, nsteps, body, 0)
    for u in range(NA_UNROLL):
        emit((nsteps - 1) * NA_UNROLL + u, ot_ref[u])


def _na_attn(q, k, v, z, table):
    b, s, _ = q.shape
    rows = s // GRID_W
    assert rows >= NA_KH
    grp = pl.BlockSpec((None, s, LANE), lambda g, bb: (bb, 0, g))
    return pl.pallas_call(
        functools.partial(_na_attn_kernel, rows=rows),
        grid=(NA_GROUPS, b),
        in_specs=[grp, grp, grp, grp,
                  pl.BlockSpec((None, NA_KH, NA_KH * GRID_W, NA_HPG * GRID_W),
                               lambda g, bb: (g, 0, 0, 0))],
        out_specs=grp,
        out_shape=jax.ShapeDtypeStruct((b, s, BRANCH), BF16),
        scratch_shapes=[pltpu.VMEM((NA_UNROLL, LANE, NA_HPG * GRID_W), F32)],
        compiler_params=_cparams(("parallel", "parallel")),
        name="na_attn",
    )(q, k, v, z, table)


def _mla_in_kernel(x_ref, g_ref, w1_ref, gq_ref, wuq_ref, gkv_ref, wuk_ref, wuvt_ref,
                   rq_ref, rk_ref, q_ref, k_ref, vt_ref, z_ref):
    h = _rmsnorm(x_ref[...], g_ref[...]).astype(BF16)
    c0, c1, c2, c3 = MLA_Q_LORA, MLA_Q_LORA + MLA_KV_LORA, MLA_Q_LORA + MLA_KV_LORA + LANE, \
        MLA_Q_LORA + MLA_KV_LORA + 2 * LANE
    cq = _dot(h, w1_ref[:, :c0])
    ckv = _dot(h, w1_ref[:, c0:c1])
    kp = _dot(h, w1_ref[:, c1:c2])
    kr = _dot(h, w1_ref[:, c2:c3])
    z_ref[...] = _dot(h, w1_ref[:, c3:]).astype(BF16)
    cqn = _rmsnorm(cq, gq_ref[...]).astype(BF16)
    ckvn = _rmsnorm(ckv, gkv_ref[...]).astype(BF16)
    rq = rq_ref[...]
    for hh in range(MLA_HEADS):
        sl = slice(hh * MLA_QK, (hh + 1) * MLA_QK)
        q_ref[:, sl] = (_dot(cqn, wuq_ref[:, sl]) * rq).astype(BF16)
    krr = (kp * rk_ref[:, :LANE] + kr * rk_ref[:, LANE:]).astype(BF16)
    kn = _dot(ckvn, wuk_ref[...]).astype(BF16)
    for hh in range(MLA_HEADS):
        k_ref[:, hh * MLA_QK:hh * MLA_QK + MLA_NOPE] = kn[:, hh * MLA_NOPE:(hh + 1) * MLA_NOPE]
        k_ref[:, hh * MLA_QK + MLA_NOPE:(hh + 1) * MLA_QK] = krr
    vt_ref[...] = _dot_nt(wuvt_ref[...], ckvn).astype(BF16)


def _mla_in(x, g, w1, gq, wuq, gkv, wuk, wuvt, rq, rk, tm):
    b, s, d = x.shape
    n1 = w1.shape[1]
    qk = MLA_HEADS * MLA_QK
    rope_spec = pl.BlockSpec((tm, 2 * LANE), lambda bb, i: (i, 0))
    return pl.pallas_call(
        _mla_in_kernel,
        grid=(b, s // tm),
        in_specs=[_tok_spec(tm, d), _full_spec((1, d)), _full_spec((d, n1)),
                  _full_spec((1, MLA_Q_LORA)), _full_spec((MLA_Q_LORA, qk)),
                  _full_spec((1, MLA_KV_LORA)), _full_spec((MLA_KV_LORA, BRANCH)),
                  _full_spec((BRANCH, MLA_KV_LORA)), rope_spec, rope_spec],
        out_specs=[_tok_spec(tm, qk), _tok_spec(tm, qk),
                   pl.BlockSpec((None, BRANCH, tm), lambda bb, i: (bb, 0, i)), _tok_spec(tm, BRANCH)],
        out_shape=[jax.ShapeDtypeStruct((b, s, qk), BF16), jax.ShapeDtypeStruct((b, s, qk), BF16),
                   jax.ShapeDtypeStruct((b, BRANCH, s), BF16), jax.ShapeDtypeStruct((b, s, BRANCH), BF16)],
        compiler_params=_cparams(("parallel", "parallel")),
        name="mla_in",
    )(x, g, w1, gq, wuq, gkv, wuk, wuvt, rq, rk)


def _mla_attn_kernel(q_ref, k_ref, vt_ref, z_ref, o_ref, *, tq, tk, nk, nq):
    items = [(t, j) for t in range(nq) for j in range(nk)]
    rows = lambda t: slice(t * tq, (t + 1) * tq)
    logits = lambda t, j: _dot_nt(k_ref[j * tk:(j + 1) * tk, :], q_ref[rows(t), :])
    st_next = logits(*items[0])
    for idx, (t, j) in enumerate(items):
        st = st_next
        if idx + 1 < len(items):
            st_next = logits(*items[idx + 1])
        if j == 0:
            m = jnp.full((1, tq), -jnp.inf, F32)
            l = jnp.zeros((1, tq), F32)
            acc = jnp.zeros((MLA_V, tq), F32)
        m_new = jnp.maximum(m, jnp.max(st, axis=0, keepdims=True))
        alpha = jnp.exp2(m - m_new)
        p = jnp.exp2(st - m_new)
        l = alpha * l + jnp.sum(p, axis=0, keepdims=True)
        acc = alpha * acc + _dot(vt_ref[:, j * tk:(j + 1) * tk], p.astype(BF16))
        m = m_new
        if j == nk - 1:
            o = (acc * (1.0 / l)).T
            o_ref[rows(t), :] = (o * _silu(z_ref[rows(t), :].astype(F32))).astype(BF16)


def _mla_attn(q, k, vt, z, tq, tk):
    b, _, s = vt.shape
    nq = min(MLA_QTILES, s // tq)
    tb = nq * tq
    return pl.pallas_call(
        functools.partial(_mla_attn_kernel, tq=tq, tk=tk, nk=s // tk, nq=nq),
        grid=(b, MLA_HEADS, s // tb),
        in_specs=[pl.BlockSpec((None, tb, MLA_QK), lambda bb, h, i: (bb, i, h)),
                  pl.BlockSpec((None, s, MLA_QK), lambda bb, h, i: (bb, 0, h)),
                  pl.BlockSpec((None, MLA_V, s), lambda bb, h, i: (bb, h, 0)),
                  pl.BlockSpec((None, tb, MLA_V), lambda bb, h, i: (bb, i, h))],
        out_specs=pl.BlockSpec((None, tb, MLA_V), lambda bb, h, i: (bb, i, h)),
        out_shape=jax.ShapeDtypeStruct((b, s, BRANCH), BF16),
        compiler_params=_cparams(("parallel", "parallel", "arbitrary")),
        name="mla_attn",
    )(q, k, vt, z)


def _hg_in_kernel(x_ref, g_ref, w_ref, lbr_ref, q_ref, lf_ref, lb_ref, i_ref, z_ref, *, li):
    h = _rmsnorm(x_ref[...], g_ref[...]).astype(BF16)
    raw = lbr_ref[...]
    e = jnp.exp(raw - jnp.max(raw, axis=0, keepdims=True))
    sm = e / jnp.sum(e, axis=0, keepdims=True)
    lower = jnp.zeros_like(sm[0])
    for dd in range(1, li + 1):
        lower = lower + sm[dd]
    q_ref[...] = _silu(_dot(h, w_ref[:, :BRANCH])).astype(BF16)
    for idx, ref in ((0, lf_ref), (1, lb_ref)):
        lb = lower[idx:idx + 1]
        raw_f = _dot(h, w_ref[:, (1 + idx) * BRANCH:(2 + idx) * BRANCH])
        ref[...] = jnp.log(lb + (1.0 - lb) * _sigmoid(raw_f)) * LOG2E
    i_ref[...] = _dot(h, w_ref[:, 3 * BRANCH:4 * BRANCH]).astype(BF16)
    z_ref[...] = _dot(h, w_ref[:, 4 * BRANCH:]).astype(BF16)


def _hg_in(x, g, w, lb_raw, li, tm):
    b, s, d = x.shape
    bf = jax.ShapeDtypeStruct((b, s, BRANCH), BF16)
    f32 = jax.ShapeDtypeStruct((b, s, BRANCH), F32)
    return pl.pallas_call(
        functools.partial(_hg_in_kernel, li=li),
        grid=(b, s // tm),
        in_specs=[_tok_spec(tm, d), _full_spec((1, d)), _full_spec((d, 5 * BRANCH)),
                  _full_spec((DEPTH, 2, BRANCH))],
        out_specs=[_tok_spec(tm, BRANCH)] * 5,
        out_shape=[bf, f32, f32, bf, bf],
        compiler_params=_cparams(("parallel", "parallel")),
        name="hg_in",
    )(x, g, w, lb_raw)


def _hg_level_operands(q, k, lf2, cum, n, fwd, sel):
    L = HG_CHUNK
    half = n // 2
    ref_row = half - 1 if fwd else half
    if n >= 16:
        zero = jnp.zeros((half, HG_DK), BF16)
        qt_parts, kt_parts = [], []
        for blk in range(L // n):
            cm = cum[blk * n + ref_row:blk * n + ref_row + 1]
            for is_upper in (False, True):
                r0 = blk * n + (half if is_upper else 0)
                seg = slice(r0, r0 + half)
                e = jnp.exp2(-jnp.abs(cum[seg] - cm))
                if is_upper == fwd:
                    qt_parts.append((q[seg] * e).astype(BF16))
                    kt_parts.append(zero)
                else:
                    qt_parts.append(zero)
                    kt_parts.append((k[seg] * e).astype(BF16))
        return jnp.concatenate(qt_parts, axis=0), jnp.concatenate(kt_parts, axis=0)
    if n == 2:
        q_side = sel["upper"][n] if fwd else jnp.logical_not(sel["upper"][n])
        gl = jnp.where(q_side, lf2, 0.0)
    else:
        c3 = cum.reshape(L // n, n, HG_DK)
        cm = jnp.broadcast_to(c3[:, ref_row:ref_row + 1, :], c3.shape).reshape(L, HG_DK)
        gl = -jnp.abs(cum - cm)
    e = jnp.exp2(gl)
    return (q * e).astype(BF16), (k * e).astype(BF16)


def _hg_chunks(items, sel):
    L = HG_CHUNK
    cats = []
    for q, lf2, v, tri3, fwd in items:
        hi = lf2.astype(BF16)
        r1 = lf2 - hi.astype(F32)
        mid = r1.astype(BF16)
        lo = (r1 - mid.astype(F32)).astype(BF16)
        cats.append(jnp.concatenate([hi, mid, lo], axis=0))
    cums = [_dot(it[3], cat) for it, cat in zip(items, cats)]
    ks = [1.0 - jnp.exp2(it[1]) for it in items]
    accs = [jnp.where(sel["eye"], jnp.sum(it[0] * k, axis=-1, keepdims=True), 0.0)
            for it, k in zip(items, ks)]
    for n in HG_LEVELS:
        ops = [_hg_level_operands(it[0], k, it[1], cum, n, it[4], sel)
               for it, k, cum in zip(items, ks, cums)]
        prods = [_dot_nt(qt, kt) for qt, kt in ops]
        accs = [jnp.where(sel["own"][it[4]][n], p, a) for it, p, a in zip(items, prods, accs)]
    o_intras = [_dot(a.astype(BF16), it[2]) for a, it in zip(accs, items)]
    edges = [cum[L - 1:L] if it[4] else cum[0:1] for it, cum in zip(items, cums)]
    khs = [(k * jnp.exp2(e - cum)).astype(BF16) for k, e, cum in zip(ks, edges, cums)]
    us = [_dot_tn(it[2], kh) for it, kh in zip(items, khs)]
    qbs = [(it[0] * jnp.exp2(cum)).astype(BF16) for it, cum in zip(items, cums)]
    return [(o, qb, u, jnp.exp2(e)) for o, qb, u, e in zip(o_intras, qbs, us, edges)]


def _hg_scan_kernel(q_ref, lf_ref, lb_ref, v_ref, z_ref, g_ref, o_ref,
                    of_ref, ob_ref, sf_ref, sb_ref, *, nchunks, fin_rows):
    L = HG_CHUNK
    t_i = lax.broadcasted_iota(jnp.int32, (L, L), 0)
    s_i = lax.broadcasted_iota(jnp.int32, (L, L), 1)
    x_i = t_i ^ s_i
    row = lax.broadcasted_iota(jnp.int32, (L, HG_DK), 0)
    sel = {"eye": t_i == s_i,
           "own": {True: {n: (x_i < n) & (x_i >= n // 2) & (t_i > s_i) for n in HG_LEVELS},
                   False: {n: (x_i < n) & (x_i >= n // 2) & (t_i < s_i) for n in HG_LEVELS}},
           "upper": {n: (row & (n - 1)) >= n // 2 for n in HG_LEVELS if n < 16}}
    t3 = lax.broadcasted_iota(jnp.int32, (L, 3 * L), 0)
    u3 = lax.broadcasted_iota(jnp.int32, (L, 3 * L), 1) & (L - 1)
    tri_f = (u3 <= t3).astype(BF16)
    tri_b = (u3 >= t3).astype(BF16)
    sf_ref[...] = jnp.zeros_like(sf_ref)
    sb_ref[...] = jnp.zeros_like(sb_ref)

    def body(i, carry):
        dirs = ((True, lf_ref, sf_ref, of_ref, tri_f), (False, lb_ref, sb_ref, ob_ref, tri_b))
        items, rows_of = [], []
        for fwd, gate_ref, _, _, tri in dirs:
            for jj in range(HG_UNROLL):
                c = i * HG_UNROLL + jj
                c = c if fwd else nchunks - 1 - c
                rows = pl.ds(pl.multiple_of(c * L, L), L)
                rows_of.append(rows)
                items.append((q_ref[rows, :].astype(F32), gate_ref[rows, :], v_ref[rows, :], tri, fwd))
        parts = _hg_chunks(items, sel)
        sts = [d[2][...] for d in dirs]
        for jj in range(HG_UNROLL):
            for di, (_, _, _, out_ref, _) in enumerate(dirs):
                o_intra, qb, u, dec = parts[di * HG_UNROLL + jj]
                out_ref[rows_of[di * HG_UNROLL + jj], :] = o_intra + _dot_nt(qb, sts[di].astype(BF16))
                sts[di] = sts[di] * dec + u
        for d, st in zip(dirs, sts):
            d[2][...] = st
        return carry

    lax.fori_loop(0, nchunks // HG_UNROLL, body, 0)

    def fin(i, carry):
        rows = pl.ds(pl.multiple_of(i * fin_rows, fin_rows), fin_rows)
        o = of_ref[rows, :] + ob_ref[rows, :]
        y = _rmsnorm(o, g_ref[...])
        o_ref[rows, :] = (y * _silu(z_ref[rows, :].astype(F32))).astype(BF16)
        return carry

    lax.fori_loop(0, (nchunks * L) // fin_rows, fin, 0)


def _hg_scan(q, lf, lb, v, z, g_out):
    b, s, _ = q.shape
    head = pl.BlockSpec((None, s, HG_DK), lambda bb, h: (bb, 0, h))
    fin_rows = min(256, s)
    return pl.pallas_call(
        functools.partial(_hg_scan_kernel, nchunks=s // HG_CHUNK, fin_rows=fin_rows),
        grid=(b, HG_HEADS),
        in_specs=[head] * 5 + [pl.BlockSpec((1, HG_DK), lambda bb, h: (0, h))],
        out_specs=head,
        out_shape=jax.ShapeDtypeStruct((b, s, BRANCH), BF16),
        scratch_shapes=[pltpu.VMEM((s, HG_DK), F32), pltpu.VMEM((s, HG_DK), F32),
                        pltpu.VMEM((HG_DK, HG_DK), F32), pltpu.VMEM((HG_DK, HG_DK), F32)],
        compiler_params=_cparams(("parallel", "parallel")),
        name="hg_scan",
    )(q, lf, lb, v, z, g_out)


def _rot_half_cols(w):
    half = MLA_ROPE // 2
    return jnp.concatenate([-w[..., half:], w[..., :half]], axis=-1)


def _prep(s, norm_g, fn_w_in, fn_w_mix, fn_w_out, na_w_in, na_rpb, na_w_out,
          mla_w_in, mla_g_q, mla_w_uq, mla_g_kv, mla_w_ukv, mla_w_out,
          hg_w_in, hg_lb_raw, hg_g_out, hg_w_out, ple_w, ple_gate_w, final_g, tk):
    bf = lambda t: t.astype(BF16)
    row = lambda t: t.reshape(1, -1).astype(F32)
    w = {}
    w["norm_g"] = [row(norm_g[i]) for i in range(DEPTH)]
    w["final_g"] = row(final_g)
    w["ple_w"] = [bf(ple_w[i]) for i in range(DEPTH)]
    w["gate_w"] = [bf(ple_gate_w[i]) for i in range(DEPTH)]
    w["w_out"] = [bf(fn_w_out[0]), bf(na_w_out[0]), bf(mla_w_out[0]), bf(hg_w_out[0])]
    w["fn_w_in"] = bf(fn_w_in[0])
    w["fn_w2"] = _fn_fold(fn_w_mix[0].astype(F32), s)
    w["dft"] = _dft_mats(s, tk)
    w["na_w_in"] = bf(na_w_in[0])
    w["na_table"] = _na_bias_table(na_rpb[0].astype(F32))
    wi = mla_w_in[0]
    c0, c1, c2 = MLA_Q_LORA, MLA_Q_LORA + MLA_KV_LORA, MLA_Q_LORA + MLA_KV_LORA + MLA_ROPE
    wkpe = wi[:, c1:c2]
    wrot = _rot_half_cols(wkpe)
    w["mla_w1"] = bf(jnp.concatenate([wi[:, :c1], wkpe, wkpe, wrot, wrot, wi[:, c2:]], axis=1))
    wq = mla_w_uq[0].reshape(MLA_Q_LORA, MLA_HEADS, MLA_NOPE + MLA_ROPE)
    wq_pe = wq[..., MLA_NOPE:]
    w["mla_wuq"] = bf(jnp.concatenate([wq[..., :MLA_NOPE], wq_pe, _rot_half_cols(wq_pe)], axis=-1)
                      .reshape(MLA_Q_LORA, MLA_HEADS * MLA_QK))
    wkv = mla_w_ukv[0].reshape(MLA_KV_LORA, MLA_HEADS, MLA_NOPE + MLA_V)
    w["mla_wuk"] = bf(wkv[..., :MLA_NOPE].reshape(MLA_KV_LORA, BRANCH))
    w["mla_wuvt"] = bf(wkv[..., MLA_NOPE:].reshape(MLA_KV_LORA, BRANCH).T)
    w["mla_gq"] = row(mla_g_q[0])
    w["mla_gkv"] = row(mla_g_kv[0])
    half = MLA_ROPE // 2
    inv = ROPE_THETA ** (-jnp.arange(half, dtype=F32) / half)
    ang = jnp.arange(s, dtype=F32)[:, None] * inv[None, :]
    cos, sin = jnp.cos(ang), jnp.sin(ang)
    cos64 = jnp.concatenate([cos, cos], axis=-1)
    sin64 = jnp.concatenate([sin, sin], axis=-1)
    scale = (MLA_NOPE + MLA_ROPE) ** -0.5 * math.log2(math.e)
    w["mla_rq"] = scale * jnp.concatenate([jnp.ones((s, MLA_NOPE), F32), cos64, sin64], axis=-1)
    w["mla_rk"] = jnp.concatenate([cos64, cos64, sin64, sin64], axis=-1)
    w["hg_w_in"] = bf(hg_w_in[0])
    w["hg_lb_raw"] = hg_lb_raw.astype(F32)
    w["hg_g_out"] = row(hg_g_out[0])
    return w


def _trunk(x, p, w, tm, tk, tq, tkv):
    tm_big = min(2 * tm, x.shape[1])

    def finish(x, o, li, final=False):
        return _out_proj(x, o, p, li, w["w_out"][li], w["gate_w"][li], w["ple_w"][li],
                         w["final_g"] if final else None, tm if o.ndim == 4 else tm_big)

    u, z = _fn_in(x, w["norm_g"][0], w["fn_w_in"], tm_big)
    o = _fn_dft(w["dft"], u, z, w["fn_w2"], tk)
    x = finish(x, o, 0)
    q, k, v, z = _na_in(x, w["norm_g"][1], w["na_w_in"], tm_big)
    o = _na_attn(q, k, v, z, w["na_table"])
    x = finish(x, o, 1)
    q, k, vt, z = _mla_in(x, w["norm_g"][2], w["mla_w1"], w["mla_gq"], w["mla_wuq"], w["mla_gkv"],
                          w["mla_wuk"], w["mla_wuvt"], w["mla_rq"], w["mla_rk"], tm)
    o = _mla_attn(q, k, vt, z, tq, tkv)
    x = finish(x, o, 2)
    q, lf, lb, v, z = _hg_in(x, w["norm_g"][3], w["hg_w_in"], w["hg_lb_raw"], 3, tm)
    o = _hg_scan(q, lf, lb, v, z, w["hg_g_out"])
    return finish(x, o, 3, final=True)


def kernel(x_prompt, x_sample, p_prompt, p_sample, norm_g, fn_w_in, fn_w_mix, fn_w_out, na_w_in, na_rpb, na_w_out, mla_w_in, mla_g_q, mla_w_uq, mla_g_kv, mla_w_ukv, mla_w_out, hg_w_in, hg_lb_raw, hg_g_out, hg_w_out, ple_w, ple_gate_w, final_g):
    s = x_prompt.shape[1]
    assert x_sample.shape[1] == s
    tm = min(512, s)
    tk = min(512, s)
    tq = min(1024, s)
    tkv = min(1024, s)
    w = _prep(s, norm_g, fn_w_in, fn_w_mix, fn_w_out, na_w_in, na_rpb, na_w_out,
              mla_w_in, mla_g_q, mla_w_uq, mla_g_kv, mla_w_ukv, mla_w_out,
              hg_w_in, hg_lb_raw, hg_g_out, hg_w_out, ple_w, ple_gate_w, final_g, tk)
    y_prompt = _trunk(x_prompt, p_prompt, w, tm, tk, tq, tkv)
    y_sample = _trunk(x_sample, p_sample, w, tm, tk, tq, tkv)
    return (y_prompt, y_sample)
```

```python
import functools
import math

import numpy as np
import jax
import jax.numpy as jnp
from jax import lax
from jax.experimental import pallas as pl
from jax.experimental.pallas import tpu as pltpu

F32 = jnp.float32
BF16 = jnp.bfloat16

D_MODEL = 1024
DEPTH = 4
BRANCH = 1024
PLE_DIM = 256
GRID_W = 64
EPS = 1e-6
LANE = 128
VMEM_LIMIT = 56 * 1024 * 1024

FN_GROUPS = 8
FN_GDIM = BRANCH // FN_GROUPS

NA_HEADS = 32
NA_HDIM = BRANCH // NA_HEADS
NA_KH = 8
NA_KW = 16
NA_HPG = LANE // NA_HDIM
NA_GROUPS = NA_HEADS // NA_HPG
NA_NEG = -1e30

MLA_HEADS = 8
MLA_NOPE = 128
MLA_ROPE = 64
MLA_V = BRANCH // MLA_HEADS
MLA_Q_LORA = 384
MLA_KV_LORA = 256
MLA_QK = MLA_NOPE + 2 * MLA_ROPE
MLA_QTILES = 4
ROPE_THETA = 10000.0

HG_HEADS = 8
HG_DK = BRANCH // HG_HEADS
HG_CHUNK = 64
HG_LEVELS = (64, 32, 16, 8, 4, 2)
HG_UNROLL = 8
NA_UNROLL = 16
LOG2E = math.log2(math.e)


def _cparams(sem):
    return pltpu.CompilerParams(dimension_semantics=sem, vmem_limit_bytes=VMEM_LIMIT)


def _rmsnorm(x, g):
    ms = jnp.mean(x * x, axis=-1, keepdims=True)
    return x * lax.rsqrt(ms + EPS) * g


def _sigmoid(x):
    return 1.0 / (1.0 + jnp.exp(-x))


def _silu(x):
    return x * _sigmoid(x)


def _dot(a, b):
    return jnp.dot(a, b, preferred_element_type=F32)


def _dot_nt(a, b):
    return lax.dot_general(a, b, (((1,), (1,)), ((), ())), preferred_element_type=F32)


def _dot_tn(a, b):
    return lax.dot_general(a, b, (((0,), (0,)), ((), ())), preferred_element_type=F32)


def _tok_spec(tm, width):
    return pl.BlockSpec((None, tm, width), lambda b, i: (b, i, 0))


def _full_spec(shape):
    nd = len(shape)
    return pl.BlockSpec(shape, lambda *_: (0,) * nd)


def _fn_in_kernel(x_ref, g_ref, w_ref, u_ref, z_ref):
    h = _rmsnorm(x_ref[...], g_ref[...]).astype(BF16)
    u_ref[...] = _dot(h, w_ref[:, :BRANCH]).astype(BF16)
    z_ref[...] = _dot(h, w_ref[:, BRANCH:]).astype(BF16)


def _fn_in(x, g, w, tm):
    b, s, d = x.shape
    return pl.pallas_call(
        _fn_in_kernel,
        grid=(b, s // tm),
        in_specs=[_tok_spec(tm, d), _full_spec((1, d)), _full_spec((d, 2 * BRANCH))],
        out_specs=[_tok_spec(tm, BRANCH)] * 2,
        out_shape=[jax.ShapeDtypeStruct((b, s, BRANCH), BF16)] * 2,
        compiler_params=_cparams(("parallel", "parallel")),
        name="fn_in",
    )(x, g, w)


def _fn_fold_kernel(c_ref, s_ref, w_ref, o_ref, *, scale):
    w = w_ref[...]
    hi = lax.Precision.HIGHEST
    wc = (scale * jnp.dot(c_ref[...], w, precision=hi, preferred_element_type=F32)).astype(BF16)
    ws = (scale * jnp.dot(s_ref[...], w, precision=hi, preferred_element_type=F32)).astype(BF16)
    o_ref[:FN_GDIM, :FN_GDIM] = wc
    o_ref[:FN_GDIM, FN_GDIM:] = wc
    o_ref[FN_GDIM:, :FN_GDIM] = -ws
    o_ref[FN_GDIM:, FN_GDIM:] = ws


def _fn_fold(w_mix, s):
    c = jnp.arange(FN_GDIM, dtype=jnp.int32)
    ang = ((c[:, None] * c[None, :]) % FN_GDIM).astype(F32) * (2.0 * math.pi / FN_GDIM)
    scale = 1.0 / math.sqrt(s * FN_GDIM)
    gspec = pl.BlockSpec((None, FN_GDIM, FN_GDIM), lambda g: (g, 0, 0))
    return pl.pallas_call(
        functools.partial(_fn_fold_kernel, scale=scale),
        grid=(FN_GROUPS,),
        in_specs=[_full_spec((FN_GDIM, FN_GDIM))] * 2 + [gspec],
        out_specs=pl.BlockSpec((None, 2 * FN_GDIM, 2 * FN_GDIM), lambda g: (g, 0, 0)),
        out_shape=jax.ShapeDtypeStruct((FN_GROUPS, 2 * FN_GDIM, 2 * FN_GDIM), BF16),
        name="fn_fold",
    )(jnp.cos(ang), jnp.sin(ang), w_mix)


def _dft_mats(s, tk):
    nt = s // 2 // tk
    k = jnp.arange(s // 2, dtype=jnp.int32)
    pos = jnp.arange(s, dtype=jnp.int32)
    ang = ((k[:, None] * pos[None, :]) % s).astype(F32) * (2.0 * math.pi / s)
    c = jnp.cos(ang).astype(BF16).reshape(nt, tk, s)
    sn = jnp.sin(ang).astype(BF16).reshape(nt, tk, s)
    nyq = jnp.where(pos % 2 == 0, 1.0, -1.0).astype(BF16)
    e = jnp.concatenate([nyq[None, :], jnp.zeros((7, s), BF16)], axis=0)
    r = np.arange(tk)
    rev = np.zeros((tk, tk), np.float32)
    rev[r[1:], tk - r[1:]] = 1.0
    return jnp.concatenate([c, sn], axis=1), e, jnp.asarray(rev, BF16)


def _fn_dft_kernel(m_ref, e_ref, rev_ref, u_ref, zd_ref, zm_ref, w2_ref, o_ref, r_ref, carry_ref, *, tk):
    u = u_ref[...]

    @pl.when(pl.program_id(1) == 0)
    def _():
        ev = _dot(e_ref[...], u).astype(BF16)
        for g in range(FN_GROUPS):
            sl = slice(g * FN_GDIM, (g + 1) * FN_GDIM)
            ab = jnp.concatenate([ev[:, sl], jnp.zeros_like(ev[:, sl])], axis=1)
            carry_ref[:, sl] = _dot(ab, w2_ref[g])[:, FN_GDIM:]

    r = _dot(m_ref[...], u)
    a = r[:tk].astype(BF16)
    bm = r[tk:].astype(BF16)
    carry = carry_ref[0:1, :]
    for g in range(FN_GROUPS):
        sl = slice(g * FN_GDIM, (g + 1) * FN_GDIM)
        ab = jnp.concatenate([a[:, sl], bm[:, sl]], axis=1)
        y2 = _dot(ab, w2_ref[g])
        o_ref[0, :, sl] = (y2[:, :FN_GDIM] * _silu(zd_ref[:, sl].astype(F32))).astype(BF16)
        r_ref[:, sl] = y2[:, FN_GDIM:].astype(BF16)
        carry_ref[:, sl] = y2[0:8, FN_GDIM:]
    mirror = _dot(rev_ref[...], r_ref[...])
    row0 = lax.broadcasted_iota(jnp.int32, (tk, BRANCH), 0) == 0
    mirror = jnp.where(row0, carry, mirror)
    o_ref[1] = (mirror * _silu(zm_ref[...].astype(F32))).astype(BF16)


def _fn_dft(mats, u, z, w2, tk):
    m, e, rev = mats
    b, s, _ = u.shape
    nt = s // 2 // tk
    return pl.pallas_call(
        functools.partial(_fn_dft_kernel, tk=tk),
        grid=(b, nt),
        in_specs=[pl.BlockSpec((None, 2 * tk, s), lambda bb, jj: (nt - 1 - jj, 0, 0)),
                  _full_spec((8, s)), _full_spec((tk, tk)),
                  pl.BlockSpec((None, s, BRANCH), lambda bb, jj: (bb, 0, 0)),
                  pl.BlockSpec((None, tk, BRANCH), lambda bb, jj: (bb, nt - 1 - jj, 0)),
                  pl.BlockSpec((None, tk, BRANCH), lambda bb, jj: (bb, nt + jj, 0)),
                  _full_spec((FN_GROUPS, 2 * FN_GDIM, 2 * FN_GDIM))],
        out_specs=pl.BlockSpec((None, 2, tk, BRANCH), lambda bb, jj: (bb, 0, nt - 1 - jj, 0)),
        out_shape=jax.ShapeDtypeStruct((b, 2, s // 2, BRANCH), BF16),
        scratch_shapes=[pltpu.VMEM((tk, BRANCH), BF16), pltpu.VMEM((8, BRANCH), F32)],
        compiler_params=_cparams(("parallel", "arbitrary")),
        name="fn_dft",
    )(m, e, rev, u, z, z, w2)


def _out_kernel(x_ref, o_ref, p_ref, wo_ref, gw_ref, pw_ref, *rest, final):
    y_ref = rest[-1]
    x1 = x_ref[...] + _dot(o_ref[...], wo_ref[...])
    gate = _sigmoid(_dot(x1.astype(BF16), gw_ref[...]))
    x2 = x1 + gate * _dot(p_ref[...].astype(BF16), pw_ref[...])
    if final:
        x2 = _rmsnorm(x2, rest[0][...])
    y_ref[...] = x2


def _out_proj(x, o, p, li, wo, gw, pw, final_g, tm):
    b, s, d = x.shape
    final = final_g is not None
    if o.ndim == 4:
        nt = s // 2 // tm
        assert o.shape == (b, 2, nt * tm, BRANCH)
        o_spec = pl.BlockSpec((None, None, tm, BRANCH),
                              lambda bb, i: (bb, i // nt, jnp.where(i < nt, i, 2 * nt - 1 - i), 0))
    else:
        o_spec = _tok_spec(tm, BRANCH)
    in_specs = [_tok_spec(tm, d), o_spec,
                pl.BlockSpec((None, None, tm, PLE_DIM), lambda bb, i: (li, bb, i, 0)),
                _full_spec((BRANCH, d)), _full_spec((d, d)), _full_spec((PLE_DIM, d))]
    args = [x, o, p, wo, gw, pw]
    if final:
        in_specs.append(_full_spec((1, d)))
        args.append(final_g)
    return pl.pallas_call(
        functools.partial(_out_kernel, final=final),
        grid=(b, s // tm),
        in_specs=in_specs,
        out_specs=_tok_spec(tm, d),
        out_shape=jax.ShapeDtypeStruct((b, s, d), F32),
        compiler_params=_cparams(("parallel", "parallel")),
        name="out_proj",
    )(*args)


def _na_in_kernel(x_ref, g_ref, w_ref, q_ref, k_ref, v_ref, z_ref):
    h = _rmsnorm(x_ref[...], g_ref[...]).astype(BF16)
    q_ref[...] = (_dot(h, w_ref[:, :BRANCH]) * (NA_HDIM ** -0.5 * LOG2E)).astype(BF16)
    k_ref[...] = _dot(h, w_ref[:, BRANCH:2 * BRANCH]).astype(BF16)
    v_ref[...] = _dot(h, w_ref[:, 2 * BRANCH:3 * BRANCH]).astype(BF16)
    z_ref[...] = _dot(h, w_ref[:, 3 * BRANCH:]).astype(BF16)


def _na_in(x, g, w, tm):
    b, s, d = x.shape
    tok = jax.ShapeDtypeStruct((b, s, BRANCH), BF16)
    return pl.pallas_call(
        _na_in_kernel,
        grid=(b, s // tm),
        in_specs=[_tok_spec(tm, d), _full_spec((1, d)), _full_spec((d, 4 * BRANCH))],
        out_specs=[_tok_spec(tm, BRANCH)] * 4,
        out_shape=[tok] * 4,
        compiler_params=_cparams(("parallel", "parallel")),
        name="na_in",
    )(x, g, w)


def _na_bias_table(rpb):
    d = np.arange(NA_KH)
    i = np.arange(NA_KH)
    ro = i[None, :] - d[:, None] + (NA_KH - 1)
    c = np.arange(GRID_W)
    kc = np.arange(GRID_W)
    win = np.clip(c - NA_KW // 2, 0, GRID_W - NA_KW)
    valid = (kc[:, None] >= win[None, :]) & (kc[:, None] < win[None, :] + NA_KW)
    pad = GRID_W - NA_KW
    r1 = jnp.pad(rpb[:, ro, :] * LOG2E, ((0, 0), (0, 0), (0, 0), (pad, pad)))
    cols = [r1[..., GRID_W - 1 - cc:2 * GRID_W - 1 - cc] for cc in range(GRID_W)]
    bias = jnp.stack(cols, axis=-1)
    bias = jnp.where(jnp.asarray(valid)[None, None, None], bias, NA_NEG)
    bias = bias.reshape(NA_GROUPS, NA_HPG, NA_KH, NA_KH, GRID_W, GRID_W)
    bias = jnp.transpose(bias, (0, 2, 3, 4, 1, 5))
    return bias.reshape(NA_GROUPS, NA_KH, NA_KH * GRID_W, NA_HPG * GRID_W).astype(F32)


def _na_attn_kernel(q_ref, k_ref, v_ref, z_ref, t_ref, o_ref, ot_ref, *, rows):
    lane = lax.broadcasted_iota(jnp.int32, (GRID_W, LANE), 1)
    head_masks = [(lane >= hh * NA_HDIM) & (lane < (hh + 1) * NA_HDIM) for hh in range(NA_HPG)]
    row_head = lax.broadcasted_iota(jnp.int32, (LANE, LANE), 0) // NA_HDIM
    low_heads = row_head < NA_HPG // 2
    even_heads = (row_head & 1) == 0
    nkeys = NA_KH * GRID_W

    def logits(r):
        rs = jnp.clip(r - NA_KH // 2, 0, rows - NA_KH)
        q_r = q_ref[pl.ds(pl.multiple_of(r * GRID_W, GRID_W), GRID_W), :]
        zero = jnp.zeros_like(q_r)
        qs = jnp.concatenate([jnp.where(m, q_r, zero) for m in head_masks], axis=0)
        kwin = k_ref[pl.ds(pl.multiple_of(rs * GRID_W, GRID_W), nkeys), :]
        return _dot_nt(kwin, qs) + t_ref[r - rs]

    def probs(sc):
        p = jnp.exp2(sc - jnp.max(sc, axis=0, keepdims=True))
        return p.astype(BF16), jnp.sum(p, axis=0, keepdims=True)

    def values(r, p, l):
        rs = jnp.clip(r - NA_KH // 2, 0, rows - NA_KH)
        vwin = v_ref[pl.ds(pl.multiple_of(rs * GRID_W, GRID_W), nkeys), :]
        return _dot_tn(vwin, p) * (1.0 / l)

    def emit(r, ot):
        w = jnp.where(low_heads, ot[:, :LANE], ot[:, LANE:])
        w = jnp.where(even_heads, w, pltpu.roll(w, GRID_W, axis=1))
        o = w.T[:GRID_W]
        q0 = pl.multiple_of(r * GRID_W, GRID_W)
        zz = z_ref[pl.ds(q0, GRID_W), :].astype(F32)
        o_ref[pl.ds(q0, GRID_W), :] = (o * _silu(zz)).astype(BF16)

    def step(i, emit_previous):
        rws = [i * NA_UNROLL + u for u in range(NA_UNROLL)]
        scs = [logits(r) for r in rws]
        if emit_previous:
            for u in range(NA_UNROLL):
                emit((i - 1) * NA_UNROLL + u, ot_ref[u])
        pls = [probs(sc) for sc in scs]
        for u, (r, (p, l)) in enumerate(zip(rws, pls)):
            ot_ref[u] = values(r, p, l)

    def body(i, carry):
        step(i, True)
        return carry

    nsteps = rows // NA_UNROLL
    step(0, False)
    lax.fori_loop(1, nsteps, body, 0)
    for u in range(NA_UNROLL):
        emit((nsteps - 1) * NA_UNROLL + u, ot_ref[u])


def _na_attn(q, k, v, z, table):
    b, s, _ = q.shape
    rows = s // GRID_W
    assert rows >= NA_KH
    grp = pl.BlockSpec((None, s, LANE), lambda g, bb: (bb, 0, g))
    return pl.pallas_call(
        functools.partial(_na_attn_kernel, rows=rows),
        grid=(NA_GROUPS, b),
        in_specs=[grp, grp, grp, grp,
                  pl.BlockSpec((None, NA_KH, NA_KH * GRID_W, NA_HPG * GRID_W),
                               lambda g, bb: (g, 0, 0, 0))],
        out_specs=grp,
        out_shape=jax.ShapeDtypeStruct((b, s, BRANCH), BF16),
        scratch_shapes=[pltpu.VMEM((NA_UNROLL, LANE, NA_HPG * GRID_W), F32)],
        compiler_params=_cparams(("parallel", "parallel")),
        name="na_attn",
    )(q, k, v, z, table)


def _mla_in_kernel(x_ref, g_ref, w1_ref, gq_ref, wuq_ref, gkv_ref, wuk_ref, wuvt_ref,
                   rq_ref, rk_ref, q_ref, k_ref, vt_ref, z_ref):
    h = _rmsnorm(x_ref[...], g_ref[...]).astype(BF16)
    c0, c1, c2, c3 = MLA_Q_LORA, MLA_Q_LORA + MLA_KV_LORA, MLA_Q_LORA + MLA_KV_LORA + LANE, \
        MLA_Q_LORA + MLA_KV_LORA + 2 * LANE
    cq = _dot(h, w1_ref[:, :c0])
    ckv = _dot(h, w1_ref[:, c0:c1])
    kp = _dot(h, w1_ref[:, c1:c2])
    kr = _dot(h, w1_ref[:, c2:c3])
    z_ref[...] = _dot(h, w1_ref[:, c3:]).astype(BF16)
    cqn = _rmsnorm(cq, gq_ref[...]).astype(BF16)
    ckvn = _rmsnorm(ckv, gkv_ref[...]).astype(BF16)
    rq = rq_ref[...]
    for hh in range(MLA_HEADS):
        sl = slice(hh * MLA_QK, (hh + 1) * MLA_QK)
        q_ref[:, sl] = (_dot(cqn, wuq_ref[:, sl]) * rq).astype(BF16)
    krr = (kp * rk_ref[:, :LANE] + kr * rk_ref[:, LANE:]).astype(BF16)
    kn = _dot(ckvn, wuk_ref[...]).astype(BF16)
    for hh in range(MLA_HEADS):
        k_ref[:, hh * MLA_QK:hh * MLA_QK + MLA_NOPE] = kn[:, hh * MLA_NOPE:(hh + 1) * MLA_NOPE]
        k_ref[:, hh * MLA_QK + MLA_NOPE:(hh + 1) * MLA_QK] = krr
    vt_ref[...] = _dot_nt(wuvt_ref[...], ckvn).astype(BF16)


def _mla_in(x, g, w1, gq, wuq, gkv, wuk, wuvt, rq, rk, tm):
    b, s, d = x.shape
    n1 = w1.shape[1]
    qk = MLA_HEADS * MLA_QK
    rope_spec = pl.BlockSpec((tm, 2 * LANE), lambda bb, i: (i, 0))
    return pl.pallas_call(
        _mla_in_kernel,
        grid=(b, s // tm),
        in_specs=[_tok_spec(tm, d), _full_spec((1, d)), _full_spec((d, n1)),
                  _full_spec((1, MLA_Q_LORA)), _full_spec((MLA_Q_LORA, qk)),
                  _full_spec((1, MLA_KV_LORA)), _full_spec((MLA_KV_LORA, BRANCH)),
                  _full_spec((BRANCH, MLA_KV_LORA)), rope_spec, rope_spec],
        out_specs=[_tok_spec(tm, qk), _tok_spec(tm, qk),
                   pl.BlockSpec((None, BRANCH, tm), lambda bb, i: (bb, 0, i)), _tok_spec(tm, BRANCH)],
        out_shape=[jax.ShapeDtypeStruct((b, s, qk), BF16), jax.ShapeDtypeStruct((b, s, qk), BF16),
                   jax.ShapeDtypeStruct((b, BRANCH, s), BF16), jax.ShapeDtypeStruct((b, s, BRANCH), BF16)],
        compiler_params=_cparams(("parallel", "parallel")),
        name="mla_in",
    )(x, g, w1, gq, wuq, gkv, wuk, wuvt, rq, rk)


def _mla_attn_kernel(q_ref, k_ref, vt_ref, z_ref, o_ref, *, tq, tk, nk, nq):
    items = [(t, j) for t in range(nq) for j in range(nk)]
    rows = lambda t: slice(t * tq, (t + 1) * tq)
    logits = lambda t, j: _dot_nt(k_ref[j * tk:(j + 1) * tk, :], q_ref[rows(t), :])
    st_next = logits(*items[0])
    for idx, (t, j) in enumerate(items):
        st = st_next
        if idx + 1 < len(items):
            st_next = logits(*items[idx + 1])
        if j == 0:
            m = jnp.full((1, tq), -jnp.inf, F32)
            l = jnp.zeros((1, tq), F32)
            acc = jnp.zeros((MLA_V, tq), F32)
        m_new = jnp.maximum(m, jnp.max(st, axis=0, keepdims=True))
        alpha = jnp.exp2(m - m_new)
        p = jnp.exp2(st - m_new)
        l = alpha * l + jnp.sum(p, axis=0, keepdims=True)
        acc = alpha * acc + _dot(vt_ref[:, j * tk:(j + 1) * tk], p.astype(BF16))
        m = m_new
        if j == nk - 1:
            o = (acc * (1.0 / l)).T
            o_ref[rows(t), :] = (o * _silu(z_ref[rows(t), :].astype(F32))).astype(BF16)


def _mla_attn(q, k, vt, z, tq, tk):
    b, _, s = vt.shape
    nq = min(MLA_QTILES, s // tq)
    tb = nq * tq
    return pl.pallas_call(
        functools.partial(_mla_attn_kernel, tq=tq, tk=tk, nk=s // tk, nq=nq),
        grid=(b, MLA_HEADS, s // tb),
        in_specs=[pl.BlockSpec((None, tb, MLA_QK), lambda bb, h, i: (bb, i, h)),
                  pl.BlockSpec((None, s, MLA_QK), lambda bb, h, i: (bb, 0, h)),
                  pl.BlockSpec((None, MLA_V, s), lambda bb, h, i: (bb, h, 0)),
                  pl.BlockSpec((None, tb, MLA_V), lambda bb, h, i: (bb, i, h))],
        out_specs=pl.BlockSpec((None, tb, MLA_V), lambda bb, h, i: (bb, i, h)),
        out_shape=jax.ShapeDtypeStruct((b, s, BRANCH), BF16),
        compiler_params=_cparams(("parallel", "parallel", "arbitrary")),
        name="mla_attn",
    )(q, k, vt, z)


def _hg_in_kernel(x_ref, g_ref, w_ref, lbr_ref, q_ref, lf_ref, lb_ref, i_ref, z_ref, *, li):
    h = _rmsnorm(x_ref[...], g_ref[...]).astype(BF16)
    raw = lbr_ref[...]
    e = jnp.exp(raw - jnp.max(raw, axis=0, keepdims=True))
    sm = e / jnp.sum(e, axis=0, keepdims=True)
    lower = jnp.zeros_like(sm[0])
    for dd in range(1, li + 1):
        lower = lower + sm[dd]
    q_ref[...] = _silu(_dot(h, w_ref[:, :BRANCH])).astype(BF16)
    for idx, ref in ((0, lf_ref), (1, lb_ref)):
        lb = lower[idx:idx + 1]
        raw_f = _dot(h, w_ref[:, (1 + idx) * BRANCH:(2 + idx) * BRANCH])
        ref[...] = jnp.log(lb + (1.0 - lb) * _sigmoid(raw_f)) * LOG2E
    i_ref[...] = _dot(h, w_ref[:, 3 * BRANCH:4 * BRANCH]).astype(BF16)
    z_ref[...] = _dot(h, w_ref[:, 4 * BRANCH:]).astype(BF16)


def _hg_in(x, g, w, lb_raw, li, tm):
    b, s, d = x.shape
    bf = jax.ShapeDtypeStruct((b, s, BRANCH), BF16)
    f32 = jax.ShapeDtypeStruct((b, s, BRANCH), F32)
    return pl.pallas_call(
        functools.partial(_hg_in_kernel, li=li),
        grid=(b, s // tm),
        in_specs=[_tok_spec(tm, d), _full_spec((1, d)), _full_spec((d, 5 * BRANCH)),
                  _full_spec((DEPTH, 2, BRANCH))],
        out_specs=[_tok_spec(tm, BRANCH)] * 5,
        out_shape=[bf, f32, f32, bf, bf],
        compiler_params=_cparams(("parallel", "parallel")),
        name="hg_in",
    )(x, g, w, lb_raw)


def _hg_level_operands(q, k, lf2, cum, n, fwd, sel):
    L = HG_CHUNK
    half = n // 2
    ref_row = half - 1 if fwd else half
    if n >= 16:
        zero = jnp.zeros((half, HG_DK), BF16)
        qt_parts, kt_parts = [], []
        for blk in range(L // n):
            cm = cum[blk * n + ref_row:blk * n + ref_row + 1]
            for is_upper in (False, True):
                r0 = blk * n + (half if is_upper else 0)
                seg = slice(r0, r0 + half)
                e = jnp.exp2(-jnp.abs(cum[seg] - cm))
                if is_upper == fwd:
                    qt_parts.append((q[seg] * e).astype(BF16))
                    kt_parts.append(zero)
                else:
                    qt_parts.append(zero)
                    kt_parts.append((k[seg] * e).astype(BF16))
        return jnp.concatenate(qt_parts, axis=0), jnp.concatenate(kt_parts, axis=0)
    if n == 2:
        q_side = sel["upper"][n] if fwd else jnp.logical_not(sel["upper"][n])
        gl = jnp.where(q_side, lf2, 0.0)
    else:
        c3 = cum.reshape(L // n, n, HG_DK)
        cm = jnp.broadcast_to(c3[:, ref_row:ref_row + 1, :], c3.shape).reshape(L, HG_DK)
        gl = -jnp.abs(cum - cm)
    e = jnp.exp2(gl)
    return (q * e).astype(BF16), (k * e).astype(BF16)


def _hg_chunks(items, sel):
    L = HG_CHUNK
    cats = []
    for q, lf2, v, tri3, fwd in items:
        hi = lf2.astype(BF16)
        r1 = lf2 - hi.astype(F32)
        mid = r1.astype(BF16)
        lo = (r1 - mid.astype(F32)).astype(BF16)
        cats.append(jnp.concatenate([hi, mid, lo], axis=0))
    cums = [_dot(it[3], cat) for it, cat in zip(items, cats)]
    ks = [1.0 - jnp.exp2(it[1]) for it in items]
    accs = [jnp.where(sel["eye"], jnp.sum(it[0] * k, axis=-1, keepdims=True), 0.0)
            for it, k in zip(items, ks)]
    for n in HG_LEVELS:
        ops = [_hg_level_operands(it[0], k, it[1], cum, n, it[4], sel)
               for it, k, cum in zip(items, ks, cums)]
        prods = [_dot_nt(qt, kt) for qt, kt in ops]
        accs = [jnp.where(sel["own"][it[4]][n], p, a) for it, p, a in zip(items, prods, accs)]
    o_intras = [_dot(a.astype(BF16), it[2]) for a, it in zip(accs, items)]
    edges = [cum[L - 1:L] if it[4] else cum[0:1] for it, cum in zip(items, cums)]
    khs = [(k * jnp.exp2(e - cum)).astype(BF16) for k, e, cum in zip(ks, edges, cums)]
    us = [_dot_tn(it[2], kh) for it, kh in zip(items, khs)]
    qbs = [(it[0] * jnp.exp2(cum)).astype(BF16) for it, cum in zip(items, cums)]
    return [(o, qb, u, jnp.exp2(e)) for o, qb, u, e in zip(o_intras, qbs, us, edges)]


def _hg_scan_kernel(q_ref, lf_ref, lb_ref, v_ref, z_ref, g_ref, o_ref,
                    of_ref, ob_ref, sf_ref, sb_ref, *, nchunks, fin_rows):
    L = HG_CHUNK
    t_i = lax.broadcasted_iota(jnp.int32, (L, L), 0)
    s_i = lax.broadcasted_iota(jnp.int32, (L, L), 1)
    x_i = t_i ^ s_i
    row = lax.broadcasted_iota(jnp.int32, (L, HG_DK), 0)
    sel = {"eye": t_i == s_i,
           "own": {True: {n: (x_i < n) & (x_i >= n // 2) & (t_i > s_i) for n in HG_LEVELS},
                   False: {n: (x_i < n) & (x_i >= n // 2) & (t_i < s_i) for n in HG_LEVELS}},
           "upper": {n: (row & (n - 1)) >= n // 2 for n in HG_LEVELS if n < 16}}
    t3 = lax.broadcasted_iota(jnp.int32, (L, 3 * L), 0)
    u3 = lax.broadcasted_iota(jnp.int32, (L, 3 * L), 1) & (L - 1)
    tri_f = (u3 <= t3).astype(BF16)
    tri_b = (u3 >= t3).astype(BF16)
    sf_ref[...] = jnp.zeros_like(sf_ref)
    sb_ref[...] = jnp.zeros_like(sb_ref)

    def body(i, carry):
        dirs = ((True, lf_ref, sf_ref, of_ref, tri_f), (False, lb_ref, sb_ref, ob_ref, tri_b))
        items, rows_of = [], []
        for fwd, gate_ref, _, _, tri in dirs:
            for jj in range(HG_UNROLL):
                c = i * HG_UNROLL + jj
                c = c if fwd else nchunks - 1 - c
                rows = pl.ds(pl.multiple_of(c * L, L), L)
                rows_of.append(rows)
                items.append((q_ref[rows, :].astype(F32), gate_ref[rows, :], v_ref[rows, :], tri, fwd))
        parts = _hg_chunks(items, sel)
        sts = [d[2][...] for d in dirs]
        for jj in range(HG_UNROLL):
            for di, (_, _, _, out_ref, _) in enumerate(dirs):
                o_intra, qb, u, dec = parts[di * HG_UNROLL + jj]
                out_ref[rows_of[di * HG_UNROLL + jj], :] = o_intra + _dot_nt(qb, sts[di].astype(BF16))
                sts[di] = sts[di] * dec + u
        for d, st in zip(dirs, sts):
            d[2][...] = st
        return carry

    lax.fori_loop(0, nchunks // HG_UNROLL, body, 0)

    def fin(i, carry):
        rows = pl.ds(pl.multiple_of(i * fin_rows, fin_rows), fin_rows)
        o = of_ref[rows, :] + ob_ref[rows, :]
        y = _rmsnorm(o, g_ref[...])
        o_ref[rows, :] = (y * _silu(z_ref[rows, :].astype(F32))).astype(BF16)
        return carry

    lax.fori_loop(0, (nchunks * L) // fin_rows, fin, 0)


def _hg_scan(q, lf, lb, v, z, g_out):
    b, s, _ = q.shape
    head = pl.BlockSpec((None, s, HG_DK), lambda bb, h: (bb, 0, h))
    fin_rows = min(256, s)
    return pl.pallas_call(
        functools.partial(_hg_scan_kernel, nchunks=s // HG_CHUNK, fin_rows=fin_rows),
        grid=(b, HG_HEADS),
        in_specs=[head] * 5 + [pl.BlockSpec((1, HG_DK), lambda bb, h: (0, h))],
        out_specs=head,
        out_shape=jax.ShapeDtypeStruct((b, s, BRANCH), BF16),
        scratch_shapes=[pltpu.VMEM((s, HG_DK), F32), pltpu.VMEM((s, HG_DK), F32),
                        pltpu.VMEM((HG_DK, HG_DK), F32), pltpu.VMEM((HG_DK, HG_DK), F32)],
        compiler_params=_cparams(("parallel", "parallel")),
        name="hg_scan",
    )(q, lf, lb, v, z, g_out)


def _rot_half_cols(w):
    half = MLA_ROPE // 2
    return jnp.concatenate([-w[..., half:], w[..., :half]], axis=-1)


def _prep(s, norm_g, fn_w_in, fn_w_mix, fn_w_out, na_w_in, na_rpb, na_w_out,
          mla_w_in, mla_g_q, mla_w_uq, mla_g_kv, mla_w_ukv, mla_w_out,
          hg_w_in, hg_lb_raw, hg_g_out, hg_w_out, ple_w, ple_gate_w, final_g, tk):
    bf = lambda t: t.astype(BF16)
    row = lambda t: t.reshape(1, -1).astype(F32)
    w = {}
    w["norm_g"] = [row(norm_g[i]) for i in range(DEPTH)]
    w["final_g"] = row(final_g)
    w["ple_w"] = [bf(ple_w[i]) for i in range(DEPTH)]
    w["gate_w"] = [bf(ple_gate_w[i]) for i in range(DEPTH)]
    w["w_out"] = [bf(fn_w_out[0]), bf(na_w_out[0]), bf(mla_w_out[0]), bf(hg_w_out[0])]
    w["fn_w_in"] = bf(fn_w_in[0])
    w["fn_w2"] = _fn_fold(fn_w_mix[0].astype(F32), s)
    w["dft"] = _dft_mats(s, tk)
    w["na_w_in"] = bf(na_w_in[0])
    w["na_table"] = _na_bias_table(na_rpb[0].astype(F32))
    wi = mla_w_in[0]
    c0, c1, c2 = MLA_Q_LORA, MLA_Q_LORA + MLA_KV_LORA, MLA_Q_LORA + MLA_KV_LORA + MLA_ROPE
    wkpe = wi[:, c1:c2]
    wrot = _rot_half_cols(wkpe)
    w["mla_w1"] = bf(jnp.concatenate([wi[:, :c1], wkpe, wkpe, wrot, wrot, wi[:, c2:]], axis=1))
    wq = mla_w_uq[0].reshape(MLA_Q_LORA, MLA_HEADS, MLA_NOPE + MLA_ROPE)
    wq_pe = wq[..., MLA_NOPE:]
    w["mla_wuq"] = bf(jnp.concatenate([wq[..., :MLA_NOPE], wq_pe, _rot_half_cols(wq_pe)], axis=-1)
                      .reshape(MLA_Q_LORA, MLA_HEADS * MLA_QK))
    wkv = mla_w_ukv[0].reshape(MLA_KV_LORA, MLA_HEADS, MLA_NOPE + MLA_V)
    w["mla_wuk"] = bf(wkv[..., :MLA_NOPE].reshape(MLA_KV_LORA, BRANCH))
    w["mla_wuvt"] = bf(wkv[..., MLA_NOPE:].reshape(MLA_KV_LORA, BRANCH).T)
    w["mla_gq"] = row(mla_g_q[0])
    w["mla_gkv"] = row(mla_g_kv[0])
    half = MLA_ROPE // 2
    inv = ROPE_THETA ** (-jnp.arange(half, dtype=F32) / half)
    ang = jnp.arange(s, dtype=F32)[:, None] * inv[None, :]
    cos, sin = jnp.cos(ang), jnp.sin(ang)
    cos64 = jnp.concatenate([cos, cos], axis=-1)
    sin64 = jnp.concatenate([sin, sin], axis=-1)
    scale = (MLA_NOPE + MLA_ROPE) ** -0.5 * math.log2(math.e)
    w["mla_rq"] = scale * jnp.concatenate([jnp.ones((s, MLA_NOPE), F32), cos64, sin64], axis=-1)
    w["mla_rk"] = jnp.concatenate([cos64, cos64, sin64, sin64], axis=-1)
    w["hg_w_in"] = bf(hg_w_in[0])
    w["hg_lb_raw"] = hg_lb_raw.astype(F32)
    w["hg_g_out"] = row(hg_g_out[0])
    return w


def _trunk(x, p, w, tm, tk, tq, tkv):
    tm_big = min(2 * tm, x.shape[1])

    def finish(x, o, li, final=False):
        return _out_proj(x, o, p, li, w["w_out"][li], w["gate_w"][li], w["ple_w"][li],
                         w["final_g"] if final else None, tm if o.ndim == 4 else tm_big)

    u, z = _fn_in(x, w["norm_g"][0], w["fn_w_in"], tm_big)
    o = _fn_dft(w["dft"], u, z, w["fn_w2"], tk)
    x = finish(x, o, 0)
    q, k, v, z = _na_in(x, w["norm_g"][1], w["na_w_in"], tm_big)
    o = _na_attn(q, k, v, z, w["na_table"])
    x = finish(x, o, 1)
    q, k, vt, z = _mla_in(x, w["norm_g"][2], w["mla_w1"], w["mla_gq"], w["mla_wuq"], w["mla_gkv"],
                          w["mla_wuk"], w["mla_wuvt"], w["mla_rq"], w["mla_rk"], tm)
    o = _mla_attn(q, k, vt, z, tq, tkv)
    x = finish(x, o, 2)
    q, lf, lb, v, z = _hg_in(x, w["norm_g"][3], w["hg_w_in"], w["hg_lb_raw"], 3, tm)
    o = _hg_scan(q, lf, lb, v, z, w["hg_g_out"])
    return finish(x, o, 3, final=True)


def kernel(x_prompt, x_sample, p_prompt, p_sample, norm_g, fn_w_in, fn_w_mix, fn_w_out, na_w_in, na_rpb, na_w_out, mla_w_in, mla_g_q, mla_w_uq, mla_g_kv, mla_w_ukv, mla_w_out, hg_w_in, hg_lb_raw, hg_g_out, hg_w_out, ple_w, ple_gate_w, final_g):
    s = x_prompt.shape[1]
    assert x_sample.shape[1] == s
    tm = min(512, s)
    tk = min(512, s)
    tq = min(1024, s)
    tkv = min(1024, s)
    w = _prep(s, norm_g, fn_w_in, fn_w_mix, fn_w_out, na_w_in, na_rpb, na_w_out,
              mla_w_in, mla_g_q, mla_w_uq, mla_g_kv, mla_w_ukv, mla_w_out,
              hg_w_in, hg_lb_raw, hg_g_out, hg_w_out, ple_w, ple_gate_w, final_g, tk)
    y_prompt = _trunk(x_prompt, p_prompt, w, tm, tk, tq, tkv)
    y_sample = _trunk(x_sample, p_sample, w, tm, tk, tq, tkv)
    return (y_prompt, y_sample)
```

```python
import functools
import math

import numpy as np
import jax
import jax.numpy as jnp
from jax import lax
from jax.experimental import pallas as pl
from jax.experimental.pallas import tpu as pltpu

F32 = jnp.float32
BF16 = jnp.bfloat16

D_MODEL = 1024
DEPTH = 4
BRANCH = 1024
PLE_DIM = 256
GRID_W = 64
EPS = 1e-6
LANE = 128
VMEM_LIMIT = 56 * 1024 * 1024

FN_GROUPS = 8
FN_GDIM = BRANCH // FN_GROUPS

NA_HEADS = 32
NA_HDIM = BRANCH // NA_HEADS
NA_KH = 8
NA_KW = 16
NA_HPG = LANE // NA_HDIM
NA_GROUPS = NA_HEADS // NA_HPG
NA_NEG = -1e30

MLA_HEADS = 8
MLA_NOPE = 128
MLA_ROPE = 64
MLA_V = BRANCH // MLA_HEADS
MLA_Q_LORA = 384
MLA_KV_LORA = 256
MLA_QK = MLA_NOPE + 2 * MLA_ROPE
MLA_QTILES = 4
ROPE_THETA = 10000.0

HG_HEADS = 8
HG_DK = BRANCH // HG_HEADS
HG_CHUNK = 64
HG_LEVELS = (64, 32, 16, 8, 4, 2)
HG_UNROLL = 8
NA_UNROLL = 8
LOG2E = math.log2(math.e)


def _cparams(sem):
    return pltpu.CompilerParams(dimension_semantics=sem, vmem_limit_bytes=VMEM_LIMIT)


def _rmsnorm(x, g):
    ms = jnp.mean(x * x, axis=-1, keepdims=True)
    return x * lax.rsqrt(ms + EPS) * g


def _sigmoid(x):
    return 1.0 / (1.0 + jnp.exp(-x))


def _silu(x):
    return x * _sigmoid(x)


def _dot(a, b):
    return jnp.dot(a, b, preferred_element_type=F32)


def _dot_nt(a, b):
    return lax.dot_general(a, b, (((1,), (1,)), ((), ())), preferred_element_type=F32)


def _dot_tn(a, b):
    return lax.dot_general(a, b, (((0,), (0,)), ((), ())), preferred_element_type=F32)


def _tok_spec(tm, width):
    return pl.BlockSpec((None, tm, width), lambda b, i: (b, i, 0))


def _full_spec(shape):
    nd = len(shape)
    return pl.BlockSpec(shape, lambda *_: (0,) * nd)


def _fn_in_kernel(x_ref, g_ref, w_ref, u_ref, z_ref):
    h = _rmsnorm(x_ref[...], g_ref[...]).astype(BF16)
    u_ref[...] = _dot(h, w_ref[:, :BRANCH]).astype(BF16)
    z_ref[...] = _dot(h, w_ref[:, BRANCH:]).astype(BF16)


def _fn_in(x, g, w, tm):
    b, s, d = x.shape
    return pl.pallas_call(
        _fn_in_kernel,
        grid=(b, s // tm),
        in_specs=[_tok_spec(tm, d), _full_spec((1, d)), _full_spec((d, 2 * BRANCH))],
        out_specs=[_tok_spec(tm, BRANCH)] * 2,
        out_shape=[jax.ShapeDtypeStruct((b, s, BRANCH), BF16)] * 2,
        compiler_params=_cparams(("parallel", "parallel")),
        name="fn_in",
    )(x, g, w)


def _fn_fold_kernel(c_ref, s_ref, w_ref, o_ref, *, scale):
    w = w_ref[...]
    hi = lax.Precision.HIGHEST
    wc = (scale * jnp.dot(c_ref[...], w, precision=hi, preferred_element_type=F32)).astype(BF16)
    ws = (scale * jnp.dot(s_ref[...], w, precision=hi, preferred_element_type=F32)).astype(BF16)
    o_ref[:FN_GDIM, :FN_GDIM] = wc
    o_ref[:FN_GDIM, FN_GDIM:] = wc
    o_ref[FN_GDIM:, :FN_GDIM] = -ws
    o_ref[FN_GDIM:, FN_GDIM:] = ws


def _fn_fold(w_mix, s):
    c = jnp.arange(FN_GDIM, dtype=jnp.int32)
    ang = ((c[:, None] * c[None, :]) % FN_GDIM).astype(F32) * (2.0 * math.pi / FN_GDIM)
    scale = 1.0 / math.sqrt(s * FN_GDIM)
    gspec = pl.BlockSpec((None, FN_GDIM, FN_GDIM), lambda g: (g, 0, 0))
    return pl.pallas_call(
        functools.partial(_fn_fold_kernel, scale=scale),
        grid=(FN_GROUPS,),
        in_specs=[_full_spec((FN_GDIM, FN_GDIM))] * 2 + [gspec],
        out_specs=pl.BlockSpec((None, 2 * FN_GDIM, 2 * FN_GDIM), lambda g: (g, 0, 0)),
        out_shape=jax.ShapeDtypeStruct((FN_GROUPS, 2 * FN_GDIM, 2 * FN_GDIM), BF16),
        name="fn_fold",
    )(jnp.cos(ang), jnp.sin(ang), w_mix)


def _dft_mats(s, tk):
    nt = s // 2 // tk
    k = jnp.arange(s // 2, dtype=jnp.int32)
    pos = jnp.arange(s, dtype=jnp.int32)
    ang = ((k[:, None] * pos[None, :]) % s).astype(F32) * (2.0 * math.pi / s)
    c = jnp.cos(ang).astype(BF16).reshape(nt, tk, s)
    sn = jnp.sin(ang).astype(BF16).reshape(nt, tk, s)
    nyq = jnp.where(pos % 2 == 0, 1.0, -1.0).astype(BF16)
    e = jnp.concatenate([nyq[None, :], jnp.zeros((7, s), BF16)], axis=0)
    r = np.arange(tk)
    rev = np.zeros((tk, tk), np.float32)
    rev[r[1:], tk - r[1:]] = 1.0
    return jnp.concatenate([c, sn], axis=1), e, jnp.asarray(rev, BF16)


def _fn_dft_kernel(m_ref, e_ref, rev_ref, u_ref, zd_ref, zm_ref, w2_ref, o_ref, r_ref, carry_ref, *, tk):
    u = u_ref[...]

    @pl.when(pl.program_id(1) == 0)
    def _():
        ev = _dot(e_ref[...], u).astype(BF16)
        for g in range(FN_GROUPS):
            sl = slice(g * FN_GDIM, (g + 1) * FN_GDIM)
            ab = jnp.concatenate([ev[:, sl], jnp.zeros_like(ev[:, sl])], axis=1)
            carry_ref[:, sl] = _dot(ab, w2_ref[g])[:, FN_GDIM:]

    r = _dot(m_ref[...], u)
    a = r[:tk].astype(BF16)
    bm = r[tk:].astype(BF16)
    carry = carry_ref[0:1, :]
    for g in range(FN_GROUPS):
        sl = slice(g * FN_GDIM, (g + 1) * FN_GDIM)
        ab = jnp.concatenate([a[:, sl], bm[:, sl]], axis=1)
        y2 = _dot(ab, w2_ref[g])
        o_ref[0, :, sl] = (y2[:, :FN_GDIM] * _silu(zd_ref[:, sl].astype(F32))).astype(BF16)
        r_ref[:, sl] = y2[:, FN_GDIM:].astype(BF16)
        carry_ref[:, sl] = y2[0:8, FN_GDIM:]
    mirror = _dot(rev_ref[...], r_ref[...])
    row0 = lax.broadcasted_iota(jnp.int32, (tk, BRANCH), 0) == 0
    mirror = jnp.where(row0, carry, mirror)
    o_ref[1] = (mirror * _silu(zm_ref[...].astype(F32))).astype(BF16)


def _fn_dft(mats, u, z, w2, tk):
    m, e, rev = mats
    b, s, _ = u.shape
    nt = s // 2 // tk
    return pl.pallas_call(
        functools.partial(_fn_dft_kernel, tk=tk),
        grid=(b, nt),
        in_specs=[pl.BlockSpec((None, 2 * tk, s), lambda bb, jj: (nt - 1 - jj, 0, 0)),
                  _full_spec((8, s)), _full_spec((tk, tk)),
                  pl.BlockSpec((None, s, BRANCH), lambda bb, jj: (bb, 0, 0)),
                  pl.BlockSpec((None, tk, BRANCH), lambda bb, jj: (bb, nt - 1 - jj, 0)),
                  pl.BlockSpec((None, tk, BRANCH), lambda bb, jj: (bb, nt + jj, 0)),
                  _full_spec((FN_GROUPS, 2 * FN_GDIM, 2 * FN_GDIM))],
        out_specs=pl.BlockSpec((None, 2, tk, BRANCH), lambda bb, jj: (bb, 0, nt - 1 - jj, 0)),
        out_shape=jax.ShapeDtypeStruct((b, 2, s // 2, BRANCH), BF16),
        scratch_shapes=[pltpu.VMEM((tk, BRANCH), BF16), pltpu.VMEM((8, BRANCH), F32)],
        compiler_params=_cparams(("parallel", "arbitrary")),
        name="fn_dft",
    )(m, e, rev, u, z, z, w2)


def _out_kernel(x_ref, o_ref, p_ref, wo_ref, gw_ref, pw_ref, *rest, final):
    y_ref = rest[-1]
    x1 = x_ref[...] + _dot(o_ref[...], wo_ref[...])
    gate = _sigmoid(_dot(x1.astype(BF16), gw_ref[...]))
    x2 = x1 + gate * _dot(p_ref[...].astype(BF16), pw_ref[...])
    if final:
        x2 = _rmsnorm(x2, rest[0][...])
    y_ref[...] = x2


def _out_proj(x, o, p, li, wo, gw, pw, final_g, tm):
    b, s, d = x.shape
    final = final_g is not None
    if o.ndim == 4:
        nt = s // 2 // tm
        assert o.shape == (b, 2, nt * tm, BRANCH)
        o_spec = pl.BlockSpec((None, None, tm, BRANCH),
                              lambda bb, i: (bb, i // nt, jnp.where(i < nt, i, 2 * nt - 1 - i), 0))
    else:
        o_spec = _tok_spec(tm, BRANCH)
    in_specs = [_tok_spec(tm, d), o_spec,
                pl.BlockSpec((None, None, tm, PLE_DIM), lambda bb, i: (li, bb, i, 0)),
                _full_spec((BRANCH, d)), _full_spec((d, d)), _full_spec((PLE_DIM, d))]
    args = [x, o, p, wo, gw, pw]
    if final:
        in_specs.append(_full_spec((1, d)))
        args.append(final_g)
    return pl.pallas_call(
        functools.partial(_out_kernel, final=final),
        grid=(b, s // tm),
        in_specs=in_specs,
        out_specs=_tok_spec(tm, d),
        out_shape=jax.ShapeDtypeStruct((b, s, d), F32),
        compiler_params=_cparams(("parallel", "parallel")),
        name="out_proj",
    )(*args)


def _na_in_kernel(x_ref, g_ref, w_ref, q_ref, k_ref, v_ref, z_ref):
    h = _rmsnorm(x_ref[...], g_ref[...]).astype(BF16)
    q_ref[...] = (_dot(h, w_ref[:, :BRANCH]) * (NA_HDIM ** -0.5 * LOG2E)).astype(BF16)
    k_ref[...] = _dot(h, w_ref[:, BRANCH:2 * BRANCH]).astype(BF16)
    v_ref[...] = _dot(h, w_ref[:, 2 * BRANCH:3 * BRANCH]).astype(BF16)
    z_ref[...] = _dot(h, w_ref[:, 3 * BRANCH:]).astype(BF16)


def _na_in(x, g, w, tm):
    b, s, d = x.shape
    tok = jax.ShapeDtypeStruct((b, s, BRANCH), BF16)
    return pl.pallas_call(
        _na_in_kernel,
        grid=(b, s // tm),
        in_specs=[_tok_spec(tm, d), _full_spec((1, d)), _full_spec((d, 4 * BRANCH))],
        out_specs=[_tok_spec(tm, BRANCH)] * 4,
        out_shape=[tok] * 4,
        compiler_params=_cparams(("parallel", "parallel")),
        name="na_in",
    )(x, g, w)


def _na_bias_table(rpb):
    d = np.arange(NA_KH)
    i = np.arange(NA_KH)
    ro = i[None, :] - d[:, None] + (NA_KH - 1)
    c = np.arange(GRID_W)
    kc = np.arange(GRID_W)
    win = np.clip(c - NA_KW // 2, 0, GRID_W - NA_KW)
    valid = (kc[:, None] >= win[None, :]) & (kc[:, None] < win[None, :] + NA_KW)
    pad = GRID_W - NA_KW
    r1 = jnp.pad(rpb[:, ro, :] * LOG2E, ((0, 0), (0, 0), (0, 0), (pad, pad)))
    cols = [r1[..., GRID_W - 1 - cc:2 * GRID_W - 1 - cc] for cc in range(GRID_W)]
    bias = jnp.stack(cols, axis=-1)
    bias = jnp.where(jnp.asarray(valid)[None, None, None], bias, NA_NEG)
    bias = bias.reshape(NA_GROUPS, NA_HPG, NA_KH, NA_KH, GRID_W, GRID_W)
    bias = jnp.transpose(bias, (0, 2, 3, 4, 1, 5))
    return bias.reshape(NA_GROUPS, NA_KH, NA_KH * GRID_W, NA_HPG * GRID_W).astype(F32)


def _na_attn_kernel(q_ref, k_ref, v_ref, z_ref, t_ref, o_ref, ot_ref, *, rows):
    lane = lax.broadcasted_iota(jnp.int32, (GRID_W, LANE), 1)
    head_masks = [(lane >= hh * NA_HDIM) & (lane < (hh + 1) * NA_HDIM) for hh in range(NA_HPG)]
    row_head = lax.broadcasted_iota(jnp.int32, (LANE, LANE), 0) // NA_HDIM
    low_heads = row_head < NA_HPG // 2
    even_heads = (row_head & 1) == 0
    nkeys = NA_KH * GRID_W

    def logits(r):
        rs = jnp.clip(r - NA_KH // 2, 0, rows - NA_KH)
        q_r = q_ref[pl.ds(pl.multiple_of(r * GRID_W, GRID_W), GRID_W), :]
        zero = jnp.zeros_like(q_r)
        qs = jnp.concatenate([jnp.where(m, q_r, zero) for m in head_masks], axis=0)
        kwin = k_ref[pl.ds(pl.multiple_of(rs * GRID_W, GRID_W), nkeys), :]
        return _dot_nt(kwin, qs) + t_ref[r - rs]

    def probs(sc):
        p = jnp.exp2(sc - jnp.max(sc, axis=0, keepdims=True))
        return p.astype(BF16), jnp.sum(p, axis=0, keepdims=True)

    def values(r, p, l):
        rs = jnp.clip(r - NA_KH // 2, 0, rows - NA_KH)
        vwin = v_ref[pl.ds(pl.multiple_of(rs * GRID_W, GRID_W), nkeys), :]
        return _dot_tn(vwin, p) * (1.0 / l)

    def emit(r, ot):
        w = jnp.where(low_heads, ot[:, :LANE], ot[:, LANE:])
        w = jnp.where(even_heads, w, pltpu.roll(w, GRID_W, axis=1))
        o = w.T[:GRID_W]
        q0 = pl.multiple_of(r * GRID_W, GRID_W)
        zz = z_ref[pl.ds(q0, GRID_W), :].astype(F32)
        o_ref[pl.ds(q0, GRID_W), :] = (o * _silu(zz)).astype(BF16)

    def step(i, emit_previous):
        rws = [i * NA_UNROLL + u for u in range(NA_UNROLL)]
        scs = [logits(r) for r in rws]
        if emit_previous:
            for u in range(NA_UNROLL):
                emit((i - 1) * NA_UNROLL + u, ot_ref[u])
        pls = [probs(sc) for sc in scs]
        for u, (r, (p, l)) in enumerate(zip(rws, pls)):
            ot_ref[u] = values(r, p, l)

    def body(i, carry):
        step(i, True)
        return carry

    nsteps = rows // NA_UNROLL
    step(0, False)
    lax.fori_loop(1, nsteps, body, 0)
    for u in range(NA_UNROLL):
        emit((nsteps - 1) * NA_UNROLL + u, ot_ref[u])


def _na_attn(q, k, v, z, table):
    b, s, _ = q.shape
    rows = s // GRID_W
    assert rows >= NA_KH
    grp = pl.BlockSpec((None, s, LANE), lambda g, bb: (bb, 0, g))
    return pl.pallas_call(
        functools.partial(_na_attn_kernel, rows=rows),
        grid=(NA_GROUPS, b),
        in_specs=[grp, grp, grp, grp,
                  pl.BlockSpec((None, NA_KH, NA_KH * GRID_W, NA_HPG * GRID_W),
                               lambda g, bb: (g, 0, 0, 0))],
        out_specs=grp,
        out_shape=jax.ShapeDtypeStruct((b, s, BRANCH), BF16),
        scratch_shapes=[pltpu.VMEM((NA_UNROLL, LANE, NA_HPG * GRID_W), F32)],
        compiler_params=_cparams(("parallel", "parallel")),
        name="na_attn",
    )(q, k, v, z, table)


def _mla_in_kernel(x_ref, g_ref, w1_ref, gq_ref, wuq_ref, gkv_ref, wuk_ref, wuvt_ref,
                   rq_ref, rk_ref, q_ref, k_ref, vt_ref, z_ref):
    h = _rmsnorm(x_ref[...], g_ref[...]).astype(BF16)
    c0, c1, c2, c3 = MLA_Q_LORA, MLA_Q_LORA + MLA_KV_LORA, MLA_Q_LORA + MLA_KV_LORA + LANE, \
        MLA_Q_LORA + MLA_KV_LORA + 2 * LANE
    cq = _dot(h, w1_ref[:, :c0])
    ckv = _dot(h, w1_ref[:, c0:c1])
    kp = _dot(h, w1_ref[:, c1:c2])
    kr = _dot(h, w1_ref[:, c2:c3])
    z_ref[...] = _dot(h, w1_ref[:, c3:]).astype(BF16)
    cqn = _rmsnorm(cq, gq_ref[...]).astype(BF16)
    ckvn = _rmsnorm(ckv, gkv_ref[...]).astype(BF16)
    rq = rq_ref[...]
    for hh in range(MLA_HEADS):
        sl = slice(hh * MLA_QK, (hh + 1) * MLA_QK)
        q_ref[:, sl] = (_dot(cqn, wuq_ref[:, sl]) * rq).astype(BF16)
    krr = (kp * rk_ref[:, :LANE] + kr * rk_ref[:, LANE:]).astype(BF16)
    kn = _dot(ckvn, wuk_ref[...]).astype(BF16)
    for hh in range(MLA_HEADS):
        k_ref[:, hh * MLA_QK:hh * MLA_QK + MLA_NOPE] = kn[:, hh * MLA_NOPE:(hh + 1) * MLA_NOPE]
        k_ref[:, hh * MLA_QK + MLA_NOPE:(hh + 1) * MLA_QK] = krr
    vt_ref[...] = _dot_nt(wuvt_ref[...], ckvn).astype(BF16)


def _mla_in(x, g, w1, gq, wuq, gkv, wuk, wuvt, rq, rk, tm):
    b, s, d = x.shape
    n1 = w1.shape[1]
    qk = MLA_HEADS * MLA_QK
    rope_spec = pl.BlockSpec((tm, 2 * LANE), lambda bb, i: (i, 0))
    return pl.pallas_call(
        _mla_in_kernel,
        grid=(b, s // tm),
        in_specs=[_tok_spec(tm, d), _full_spec((1, d)), _full_spec((d, n1)),
                  _full_spec((1, MLA_Q_LORA)), _full_spec((MLA_Q_LORA, qk)),
                  _full_spec((1, MLA_KV_LORA)), _full_spec((MLA_KV_LORA, BRANCH)),
                  _full_spec((BRANCH, MLA_KV_LORA)), rope_spec, rope_spec],
        out_specs=[_tok_spec(tm, qk), _tok_spec(tm, qk),
                   pl.BlockSpec((None, BRANCH, tm), lambda bb, i: (bb, 0, i)), _tok_spec(tm, BRANCH)],
        out_shape=[jax.ShapeDtypeStruct((b, s, qk), BF16), jax.ShapeDtypeStruct((b, s, qk), BF16),
                   jax.ShapeDtypeStruct((b, BRANCH, s), BF16), jax.ShapeDtypeStruct((b, s, BRANCH), BF16)],
        compiler_params=_cparams(("parallel", "parallel")),
        name="mla_in",
    )(x, g, w1, gq, wuq, gkv, wuk, wuvt, rq, rk)


def _mla_attn_kernel(q_ref, k_ref, vt_ref, z_ref, o_ref, *, tq, tk, nk, nq):
    items = [(t, j) for t in range(nq) for j in range(nk)]
    rows = lambda t: slice(t * tq, (t + 1) * tq)
    logits = lambda t, j: _dot_nt(k_ref[j * tk:(j + 1) * tk, :], q_ref[rows(t), :])
    st_next = logits(*items[0])
    for idx, (t, j) in enumerate(items):
        st = st_next
        if idx + 1 < len(items):
            st_next = logits(*items[idx + 1])
        if j == 0:
            m = jnp.full((1, tq), -jnp.inf, F32)
            l = jnp.zeros((1, tq), F32)
            acc = jnp.zeros((MLA_V, tq), F32)
        m_new = jnp.maximum(m, jnp.max(st, axis=0, keepdims=True))
        alpha = jnp.exp2(m - m_new)
        p = jnp.exp2(st - m_new)
        l = alpha * l + jnp.sum(p, axis=0, keepdims=True)
        acc = alpha * acc + _dot(vt_ref[:, j * tk:(j + 1) * tk], p.astype(BF16))
        m = m_new
        if j == nk - 1:
            o = (acc * (1.0 / l)).T
            o_ref[rows(t), :] = (o * _silu(z_ref[rows(t), :].astype(F32))).astype(BF16)


def _mla_attn(q, k, vt, z, tq, tk):
    b, _, s = vt.shape
    nq = min(MLA_QTILES, s // tq)
    tb = nq * tq
    return pl.pallas_call(
        functools.partial(_mla_attn_kernel, tq=tq, tk=tk, nk=s // tk, nq=nq),
        grid=(b, MLA_HEADS, s // tb),
        in_specs=[pl.BlockSpec((None, tb, MLA_QK), lambda bb, h, i: (bb, i, h)),
                  pl.BlockSpec((None, s, MLA_QK), lambda bb, h, i: (bb, 0, h)),
                  pl.BlockSpec((None, MLA_V, s), lambda bb, h, i: (bb, h, 0)),
                  pl.BlockSpec((None, tb, MLA_V), lambda bb, h, i: (bb, i, h))],
        out_specs=pl.BlockSpec((None, tb, MLA_V), lambda bb, h, i: (bb, i, h)),
        out_shape=jax.ShapeDtypeStruct((b, s, BRANCH), BF16),
        compiler_params=_cparams(("parallel", "parallel", "arbitrary")),
        name="mla_attn",
    )(q, k, vt, z)


def _hg_in_kernel(x_ref, g_ref, w_ref, lbr_ref, q_ref, lf_ref, lb_ref, i_ref, z_ref, *, li):
    h = _rmsnorm(x_ref[...], g_ref[...]).astype(BF16)
    raw = lbr_ref[...]
    e = jnp.exp(raw - jnp.max(raw, axis=0, keepdims=True))
    sm = e / jnp.sum(e, axis=0, keepdims=True)
    lower = jnp.zeros_like(sm[0])
    for dd in range(1, li + 1):
        lower = lower + sm[dd]
    q_ref[...] = _silu(_dot(h, w_ref[:, :BRANCH])).astype(BF16)
    for idx, ref in ((0, lf_ref), (1, lb_ref)):
        lb = lower[idx:idx + 1]
        raw_f = _dot(h, w_ref[:, (1 + idx) * BRANCH:(2 + idx) * BRANCH])
        ref[...] = jnp.log(lb + (1.0 - lb) * _sigmoid(raw_f)) * LOG2E
    i_ref[...] = _dot(h, w_ref[:, 3 * BRANCH:4 * BRANCH]).astype(BF16)
    z_ref[...] = _dot(h, w_ref[:, 4 * BRANCH:]).astype(BF16)


def _hg_in(x, g, w, lb_raw, li, tm):
    b, s, d = x.shape
    bf = jax.ShapeDtypeStruct((b, s, BRANCH), BF16)
    f32 = jax.ShapeDtypeStruct((b, s, BRANCH), F32)
    return pl.pallas_call(
        functools.partial(_hg_in_kernel, li=li),
        grid=(b, s // tm),
        in_specs=[_tok_spec(tm, d), _full_spec((1, d)), _full_spec((d, 5 * BRANCH)),
                  _full_spec((DEPTH, 2, BRANCH))],
        out_specs=[_tok_spec(tm, BRANCH)] * 5,
        out_shape=[bf, f32, f32, bf, bf],
        compiler_params=_cparams(("parallel", "parallel")),
        name="hg_in",
    )(x, g, w, lb_raw)


def _hg_level_operands(q, k, lf2, cum, n, fwd, sel):
    L = HG_CHUNK
    half = n // 2
    ref_row = half - 1 if fwd else half
    if n >= 16:
        zero = jnp.zeros((half, HG_DK), BF16)
        qt_parts, kt_parts = [], []
        for blk in range(L // n):
            cm = cum[blk * n + ref_row:blk * n + ref_row + 1]
            for is_upper in (False, True):
                r0 = blk * n + (half if is_upper else 0)
                seg = slice(r0, r0 + half)
                e = jnp.exp2(-jnp.abs(cum[seg] - cm))
                if is_upper == fwd:
                    qt_parts.append((q[seg] * e).astype(BF16))
                    kt_parts.append(zero)
                else:
                    qt_parts.append(zero)
                    kt_parts.append((k[seg] * e).astype(BF16))
        return jnp.concatenate(qt_parts, axis=0), jnp.concatenate(kt_parts, axis=0)
    if n == 2:
        q_side = sel["upper"][n] if fwd else jnp.logical_not(sel["upper"][n])
        gl = jnp.where(q_side, lf2, 0.0)
    else:
        c3 = cum.reshape(L // n, n, HG_DK)
        cm = jnp.broadcast_to(c3[:, ref_row:ref_row + 1, :], c3.shape).reshape(L, HG_DK)
        gl = -jnp.abs(cum - cm)
    e = jnp.exp2(gl)
    return (q * e).astype(BF16), (k * e).astype(BF16)


def _hg_chunks(items, sel):
    L = HG_CHUNK
    cats = []
    for q, lf2, v, tri3, fwd in items:
        hi = lf2.astype(BF16)
        r1 = lf2 - hi.astype(F32)
        mid = r1.astype(BF16)
        lo = (r1 - mid.astype(F32)).astype(BF16)
        cats.append(jnp.concatenate([hi, mid, lo], axis=0))
    cums = [_dot(it[3], cat) for it, cat in zip(items, cats)]
    ks = [1.0 - jnp.exp2(it[1]) for it in items]
    accs = [jnp.where(sel["eye"], jnp.sum(it[0] * k, axis=-1, keepdims=True), 0.0)
            for it, k in zip(items, ks)]
    for n in HG_LEVELS:
        ops = [_hg_level_operands(it[0], k, it[1], cum, n, it[4], sel)
               for it, k, cum in zip(items, ks, cums)]
        prods = [_dot_nt(qt, kt) for qt, kt in ops]
        accs = [jnp.where(sel["own"][it[4]][n], p, a) for it, p, a in zip(items, prods, accs)]
    o_intras = [_dot(a.astype(BF16), it[2]) for a, it in zip(accs, items)]
    edges = [cum[L - 1:L] if it[4] else cum[0:1] for it, cum in zip(items, cums)]
    khs = [(k * jnp.exp2(e - cum)).astype(BF16) for k, e, cum in zip(ks, edges, cums)]
    us = [_dot_tn(it[2], kh) for it, kh in zip(items, khs)]
    qbs = [(it[0] * jnp.exp2(cum)).astype(BF16) for it, cum in zip(items, cums)]
    return [(o, qb, u, jnp.exp2(e)) for o, qb, u, e in zip(o_intras, qbs, us, edges)]


def _hg_scan_kernel(q_ref, lf_ref, lb_ref, v_ref, z_ref, g_ref, o_ref,
                    of_ref, ob_ref, sf_ref, sb_ref, *, nchunks, fin_rows):
    L = HG_CHUNK
    t_i = lax.broadcasted_iota(jnp.int32, (L, L), 0)
    s_i = lax.broadcasted_iota(jnp.int32, (L, L), 1)
    x_i = t_i ^ s_i
    row = lax.broadcasted_iota(jnp.int32, (L, HG_DK), 0)
    sel = {"eye": t_i == s_i,
           "own": {True: {n: (x_i < n) & (x_i >= n // 2) & (t_i > s_i) for n in HG_LEVELS},
                   False: {n: (x_i < n) & (x_i >= n // 2) & (t_i < s_i) for n in HG_LEVELS}},
           "upper": {n: (row & (n - 1)) >= n // 2 for n in HG_LEVELS if n < 16}}
    t3 = lax.broadcasted_iota(jnp.int32, (L, 3 * L), 0)
    u3 = lax.broadcasted_iota(jnp.int32, (L, 3 * L), 1) & (L - 1)
    tri_f = (u3 <= t3).astype(BF16)
    tri_b = (u3 >= t3).astype(BF16)
    sf_ref[...] = jnp.zeros_like(sf_ref)
    sb_ref[...] = jnp.zeros_like(sb_ref)

    def body(i, carry):
        dirs = ((True, lf_ref, sf_ref, of_ref, tri_f), (False, lb_ref, sb_ref, ob_ref, tri_b))
        items, rows_of = [], []
        for fwd, gate_ref, _, _, tri in dirs:
            for jj in range(HG_UNROLL):
                c = i * HG_UNROLL + jj
                c = c if fwd else nchunks - 1 - c
                rows = pl.ds(pl.multiple_of(c * L, L), L)
                rows_of.append(rows)
                items.append((q_ref[rows, :].astype(F32), gate_ref[rows, :], v_ref[rows, :], tri, fwd))
        parts = _hg_chunks(items, sel)
        sts = [d[2][...] for d in dirs]
        for jj in range(HG_UNROLL):
            for di, (_, _, _, out_ref, _) in enumerate(dirs):
                o_intra, qb, u, dec = parts[di * HG_UNROLL + jj]
                out_ref[rows_of[di * HG_UNROLL + jj], :] = o_intra + _dot_nt(qb, sts[di].astype(BF16))
                sts[di] = sts[di] * dec + u
        for d, st in zip(dirs, sts):
            d[2][...] = st
        return carry

    lax.fori_loop(0, nchunks // HG_UNROLL, body, 0)

    def fin(i, carry):
        rows = pl.ds(pl.multiple_of(i * fin_rows, fin_rows), fin_rows)
        o = of_ref[rows, :] + ob_ref[rows, :]
        y = _rmsnorm(o, g_ref[...])
        o_ref[rows, :] = (y * _silu(z_ref[rows, :].astype(F32))).astype(BF16)
        return carry

    lax.fori_loop(0, (nchunks * L) // fin_rows, fin, 0)


def _hg_scan(q, lf, lb, v, z, g_out):
    b, s, _ = q.shape
    head = pl.BlockSpec((None, s, HG_DK), lambda bb, h: (bb, 0, h))
    fin_rows = min(1024, s)
    return pl.pallas_call(
        functools.partial(_hg_scan_kernel, nchunks=s // HG_CHUNK, fin_rows=fin_rows),
        grid=(b, HG_HEADS),
        in_specs=[head] * 5 + [pl.BlockSpec((1, HG_DK), lambda bb, h: (0, h))],
        out_specs=head,
        out_shape=jax.ShapeDtypeStruct((b, s, BRANCH), BF16),
        scratch_shapes=[pltpu.VMEM((s, HG_DK), F32), pltpu.VMEM((s, HG_DK), F32),
                        pltpu.VMEM((HG_DK, HG_DK), F32), pltpu.VMEM((HG_DK, HG_DK), F32)],
        compiler_params=_cparams(("parallel", "parallel")),
        name="hg_scan",
    )(q, lf, lb, v, z, g_out)


def _rot_half_cols(w):
    half = MLA_ROPE // 2
    return jnp.concatenate([-w[..., half:], w[..., :half]], axis=-1)


def _prep(s, norm_g, fn_w_in, fn_w_mix, fn_w_out, na_w_in, na_rpb, na_w_out,
          mla_w_in, mla_g_q, mla_w_uq, mla_g_kv, mla_w_ukv, mla_w_out,
          hg_w_in, hg_lb_raw, hg_g_out, hg_w_out, ple_w, ple_gate_w, final_g, tk):
    bf = lambda t: t.astype(BF16)
    row = lambda t: t.reshape(1, -1).astype(F32)
    w = {}
    w["norm_g"] = [row(norm_g[i]) for i in range(DEPTH)]
    w["final_g"] = row(final_g)
    w["ple_w"] = [bf(ple_w[i]) for i in range(DEPTH)]
    w["gate_w"] = [bf(ple_gate_w[i]) for i in range(DEPTH)]
    w["w_out"] = [bf(fn_w_out[0]), bf(na_w_out[0]), bf(mla_w_out[0]), bf(hg_w_out[0])]
    w["fn_w_in"] = bf(fn_w_in[0])
    w["fn_w2"] = _fn_fold(fn_w_mix[0].astype(F32), s)
    w["dft"] = _dft_mats(s, tk)
    w["na_w_in"] = bf(na_w_in[0])
    w["na_table"] = _na_bias_table(na_rpb[0].astype(F32))
    wi = mla_w_in[0]
    c0, c1, c2 = MLA_Q_LORA, MLA_Q_LORA + MLA_KV_LORA, MLA_Q_LORA + MLA_KV_LORA + MLA_ROPE
    wkpe = wi[:, c1:c2]
    wrot = _rot_half_cols(wkpe)
    w["mla_w1"] = bf(jnp.concatenate([wi[:, :c1], wkpe, wkpe, wrot, wrot, wi[:, c2:]], axis=1))
    wq = mla_w_uq[0].reshape(MLA_Q_LORA, MLA_HEADS, MLA_NOPE + MLA_ROPE)
    wq_pe = wq[..., MLA_NOPE:]
    w["mla_wuq"] = bf(jnp.concatenate([wq[..., :MLA_NOPE], wq_pe, _rot_half_cols(wq_pe)], axis=-1)
                      .reshape(MLA_Q_LORA, MLA_HEADS * MLA_QK))
    wkv = mla_w_ukv[0].reshape(MLA_KV_LORA, MLA_HEADS, MLA_NOPE + MLA_V)
    w["mla_wuk"] = bf(wkv[..., :MLA_NOPE].reshape(MLA_KV_LORA, BRANCH))
    w["mla_wuvt"] = bf(wkv[..., MLA_NOPE:].reshape(MLA_KV_LORA, BRANCH).T)
    w["mla_gq"] = row(mla_g_q[0])
    w["mla_gkv"] = row(mla_g_kv[0])
    half = MLA_ROPE // 2
    inv = ROPE_THETA ** (-jnp.arange(half, dtype=F32) / half)
    ang = jnp.arange(s, dtype=F32)[:, None] * inv[None, :]
    cos, sin = jnp.cos(ang), jnp.sin(ang)
    cos64 = jnp.concatenate([cos, cos], axis=-1)
    sin64 = jnp.concatenate([sin, sin], axis=-1)
    scale = (MLA_NOPE + MLA_ROPE) ** -0.5 * math.log2(math.e)
    w["mla_rq"] = scale * jnp.concatenate([jnp.ones((s, MLA_NOPE), F32), cos64, sin64], axis=-1)
    w["mla_rk"] = jnp.concatenate([cos64, cos64, sin64, sin64], axis=-1)
    w["hg_w_in"] = bf(hg_w_in[0])
    w["hg_lb_raw"] = hg_lb_raw.astype(F32)
    w["hg_g_out"] = row(hg_g_out[0])
    return w


def _trunk(x, p, w, tm, tk, tq, tkv):
    tm_big = min(2 * tm, x.shape[1])

    def finish(x, o, li, final=False):
        return _out_proj(x, o, p, li, w["w_out"][li], w["gate_w"][li], w["ple_w"][li],
                         w["final_g"] if final else None, tm if o.ndim == 4 else tm_big)

    u, z = _fn_in(x, w["norm_g"][0], w["fn_w_in"], tm_big)
    o = _fn_dft(w["dft"], u, z, w["fn_w2"], tk)
    x = finish(x, o, 0)
    q, k, v, z = _na_in(x, w["norm_g"][1], w["na_w_in"], tm_big)
    o = _na_attn(q, k, v, z, w["na_table"])
    x = finish(x, o, 1)
    q, k, vt, z = _mla_in(x, w["norm_g"][2], w["mla_w1"], w["mla_gq"], w["mla_wuq"], w["mla_gkv"],
                          w["mla_wuk"], w["mla_wuvt"], w["mla_rq"], w["mla_rk"], tm)
    o = _mla_attn(q, k, vt, z, tq, tkv)
    x = finish(x, o, 2)
    q, lf, lb, v, z = _hg_in(x, w["norm_g"][3], w["hg_w_in"], w["hg_lb_raw"], 3, tm)
    o = _hg_scan(q, lf, lb, v, z, w["hg_g_out"])
    return finish(x, o, 3, final=True)


def kernel(x_prompt, x_sample, p_prompt, p_sample, norm_g, fn_w_in, fn_w_mix, fn_w_out, na_w_in, na_rpb, na_w_out, mla_w_in, mla_g_q, mla_w_uq, mla_g_kv, mla_w_ukv, mla_w_out, hg_w_in, hg_lb_raw, hg_g_out, hg_w_out, ple_w, ple_gate_w, final_g):
    s = x_prompt.shape[1]
    assert x_sample.shape[1] == s
    tm = min(512, s)
    tk = min(512, s)
    tq = min(1024, s)
    tkv = min(1024, s)
    w = _prep(s, norm_g, fn_w_in, fn_w_mix, fn_w_out, na_w_in, na_rpb, na_w_out,
              mla_w_in, mla_g_q, mla_w_uq, mla_g_kv, mla_w_ukv, mla_w_out,
              hg_w_in, hg_lb_raw, hg_g_out, hg_w_out, ple_w, ple_gate_w, final_g, tk)
    y_prompt = _trunk(x_prompt, p_prompt, w, tm, tk, tq, tkv)
    y_sample = _trunk(x_sample, p_sample, w, tm, tk, tq, tkv)
    return (y_prompt, y_sample)
```
